```python
import math
import jax
import jax.numpy as jnp
from jax import lax
import numpy as np


D_MODEL = 2048
BATCH = 4
SEQ = 2048
DEPTH = 4

N_MIXERS = 3
N_META = 16
GRID_W = 64
NA_WIN_ROWS = 8
NA_WIN_COLS = 16
Q_BLOCK = 128

A_HEADS = 16
A_HEAD_DIM = D_MODEL // A_HEADS
A_WIDTH = A_HEADS * A_HEAD_DIM
B_HEADS = 8
B_HEAD_DIM = D_MODEL // (2 * B_HEADS)
B_WIDTH = 2 * B_HEADS * B_HEAD_DIM
C_HEADS = 16
C_KV_HEADS = 4
C_HEAD_DIM = D_MODEL // C_HEADS
C_WIDTH = C_HEADS * C_HEAD_DIM
C_WINDOW = 128

N_A = (DEPTH + 2) // 3
N_B = (DEPTH + 1) // 3
N_C = DEPTH // 3

DEEPNORM_ALPHA = (2 * DEPTH) ** 0.25
DEEPNORM_BETA = (8 * DEPTH) ** -0.25
LN_EPS = 1e-5
RMS_EPS = 1e-5
NEG_INF = -1e30

kernel_name = 'hybrid_na_diff_swa_deepnorm_encoder'

F32 = jnp.float32


def _layer_norm(x, g, b):
    xf = x.astype(F32)
    mu = jnp.mean(xf, -1, keepdims=True)
    var = jnp.mean(jnp.square(xf - mu), -1, keepdims=True)
    y = (xf - mu) * lax.rsqrt(var + LN_EPS) * g.astype(F32) + b.astype(F32)
    return y.astype(x.dtype)


def _alibi_slopes(n_heads):
    return jnp.exp2(-8.0 * jnp.arange(1, n_heads + 1, dtype=F32) / n_heads)


def _na_branch(h, w_in, rpb):
    Bn, L, _ = h.shape
    S = L - N_META
    rows = S // GRID_W
    wr = min(NA_WIN_ROWS, rows)
    H, dh = A_HEADS, A_HEAD_DIM
    proj = jnp.einsum('bld,de->ble', h, w_in)
    q, k, v, z = jnp.split(proj, 4, axis=-1)
    q = q.reshape(Bn, L, H, dh) * (dh ** -0.5)
    k = k.reshape(Bn, L, H, dh)
    v = v.reshape(Bn, L, H, dh)
    qm, km, vm = q[:, :N_META], k[:, :N_META], v[:, :N_META]
    s_mm = jnp.einsum('bqhd,bkhd->bhqk', qm, km).astype(F32)
    p_mm = jax.nn.softmax(s_mm, -1).astype(v.dtype)
    o_meta = jnp.einsum('bhqk,bkhd->bqhd', p_mm, vm)
    qg = q[:, N_META:].reshape(Bn, rows, GRID_W, H, dh)
    kg = k[:, N_META:].reshape(Bn, rows, GRID_W, H, dh)
    vg = v[:, N_META:].reshape(Bn, rows, GRID_W, H, dh)
    row_start = jnp.clip(jnp.arange(rows) - wr // 2, 0, rows - wr)
    col_pos = jnp.arange(GRID_W)
    col_start = jnp.clip(col_pos - NA_WIN_COLS // 2, 0, GRID_W - NA_WIN_COLS)
    col_idx = col_start[:, None] + jnp.arange(NA_WIN_COLS)[None]
    rpb_c = rpb[:, :, col_idx - col_pos[:, None] + NA_WIN_COLS - 1]

    def row_block(r):
        r0 = row_start[r]
        k_rows = lax.dynamic_slice_in_dim(kg, r0, wr, axis=1)
        v_rows = lax.dynamic_slice_in_dim(vg, r0, wr, axis=1)
        k_win = k_rows[:, :, col_idx]
        v_win = v_rows[:, :, col_idx]
        q_row = lax.dynamic_index_in_dim(qg, r, axis=1, keepdims=False)
        dr = r0 + jnp.arange(wr) - r
        bias = jnp.transpose(rpb_c[:, dr + NA_WIN_ROWS - 1], (0, 2, 1, 3))
        s_win = jnp.einsum('bqhd,biqjhd->bhqij', q_row, k_win).astype(F32) + bias[None].astype(F32)
        s_meta = jnp.einsum('bqhd,bkhd->bhqk', q_row, km).astype(F32)
        logits = jnp.concatenate([s_meta, s_win.reshape(Bn, H, GRID_W, wr * NA_WIN_COLS)], -1)
        p = jax.nn.softmax(logits, -1).astype(v.dtype)
        p_meta = p[..., :N_META]
        p_win = p[..., N_META:].reshape(Bn, H, GRID_W, wr, NA_WIN_COLS)
        return (jnp.einsum('bhqk,bkhd->bqhd', p_meta, vm)
                + jnp.einsum('bhqij,biqjhd->bqhd', p_win, v_win))

    o_grid = lax.map(row_block, jnp.arange(rows))
    o_grid = jnp.moveaxis(o_grid, 0, 1).reshape(Bn, S, H, dh)
    o = jnp.concatenate([o_meta, o_grid], 1).reshape(Bn, L, A_WIDTH)
    return o * jax.nn.silu(z)


def _diff_branch(h, w_in, lq1, lk1, lq2, lk2, subln_g, layer_idx):
    Bn, L, _ = h.shape
    S = L - N_META
    H, dh = B_HEADS, B_HEAD_DIM
    lambda_init = 0.8 - 0.6 * math.exp(-0.3 * layer_idx)
    proj = jnp.einsum('bld,de->ble', h, w_in)
    q, k, v, z = jnp.split(proj, 4, axis=-1)
    q = q.reshape(Bn, L, H, 2, dh) * (dh ** -0.5)
    k = k.reshape(Bn, L, H, 2, dh)
    v = v.reshape(Bn, L, H, 2 * dh)
    lam = (jnp.exp(jnp.sum(lq1.astype(F32) * lk1.astype(F32)))
           - jnp.exp(jnp.sum(lq2.astype(F32) * lk2.astype(F32))) + lambda_init)
    slopes = _alibi_slopes(H)
    key_pos = jnp.arange(S)

    def attend(q_blk, bias):
        s = jnp.einsum('bqhjd,bkhjd->bhjqk', q_blk, k).astype(F32) + bias[None, :, None]
        p = jax.nn.softmax(s, -1)
        a = p[:, :, 0] - lam * p[:, :, 1]
        return jnp.einsum('bhqk,bkhe->bqhe', a.astype(v.dtype), v)

    o_meta = attend(q[:, :N_META], jnp.zeros((H, N_META, L), F32))
    n_blk = S // Q_BLOCK
    q_real = q[:, N_META:].reshape(Bn, n_blk, Q_BLOCK, H, 2, dh)

    def block(n):
        q_blk = lax.dynamic_index_in_dim(q_real, n, axis=1, keepdims=False)
        q_pos = n * Q_BLOCK + jnp.arange(Q_BLOCK)
        dist = jnp.abs(q_pos[:, None] - key_pos[None]).astype(F32)
        bias = jnp.concatenate([jnp.zeros((H, Q_BLOCK, N_META), F32),
                                -slopes[:, None, None] * dist[None]], -1)
        return attend(q_blk, bias)

    o_real = lax.map(block, jnp.arange(n_blk))
    o_real = jnp.moveaxis(o_real, 0, 1).reshape(Bn, S, H, 2 * dh)
    o = jnp.concatenate([o_meta, o_real], 1).astype(F32)
    o = o * lax.rsqrt(jnp.mean(jnp.square(o), -1, keepdims=True) + RMS_EPS) * subln_g.astype(F32) * (1.0 - lambda_init)
    o = o.astype(h.dtype).reshape(Bn, L, B_WIDTH)
    return o * jax.nn.silu(z)


def _swa_branch(h, w_in, sink):
    Bn, L, _ = h.shape
    S = L - N_META
    Hk, dh = C_KV_HEADS, C_HEAD_DIM
    G = C_HEADS // C_KV_HEADS
    proj = jnp.einsum('bld,de->ble', h, w_in)
    q, k, v, z = jnp.split(proj, [C_WIDTH, C_WIDTH + Hk * dh, C_WIDTH + 2 * Hk * dh], axis=-1)
    q = q.reshape(Bn, L, Hk, G, dh) * (dh ** -0.5)
    k = k.reshape(Bn, L, Hk, dh)
    v = v.reshape(Bn, L, Hk, dh)
    km, vm = k[:, :N_META], v[:, :N_META]
    sink_l = sink.astype(F32).reshape(Hk, G)
    slopes = _alibi_slopes(C_HEADS).reshape(Hk, G)

    def attend(q_blk, k_blk, v_blk, bias):
        s = jnp.einsum('bqkgd,bskd->bkgqs', q_blk, k_blk).astype(F32) + bias[None]
        sk = jnp.broadcast_to(sink_l[None, :, :, None, None], s.shape[:-1] + (1,))
        p = jax.nn.softmax(jnp.concatenate([s, sk], -1), -1)[..., :-1]
        return jnp.einsum('bkgqs,bskd->bqkgd', p.astype(v_blk.dtype), v_blk)

    o_meta = attend(q[:, :N_META], km, vm, jnp.zeros((Hk, G, N_META, N_META), F32))
    n_blk = S // Q_BLOCK
    span = Q_BLOCK + 2 * C_WINDOW
    pad = ((0, 0), (C_WINDOW, C_WINDOW), (0, 0), (0, 0))
    k_pad = jnp.pad(k[:, N_META:], pad)
    v_pad = jnp.pad(v[:, N_META:], pad)
    q_real = q[:, N_META:].reshape(Bn, n_blk, Q_BLOCK, Hk, G, dh)

    def block(n):
        q0 = n * Q_BLOCK
        q_blk = lax.dynamic_index_in_dim(q_real, n, axis=1, keepdims=False)
        k_win = lax.dynamic_slice_in_dim(k_pad, q0, span, axis=1)
        v_win = lax.dynamic_slice_in_dim(v_pad, q0, span, axis=1)
        q_pos = q0 + jnp.arange(Q_BLOCK)
        k_pos = q0 - C_WINDOW + jnp.arange(span)
        dist = jnp.abs(q_pos[:, None] - k_pos[None])
        valid = (dist <= C_WINDOW) & (k_pos[None] >= 0) & (k_pos[None] < S)
        win_bias = jnp.where(valid[None, None],
                             -slopes[:, :, None, None] * dist.astype(F32)[None, None], NEG_INF)
        bias = jnp.concatenate([jnp.zeros((Hk, G, Q_BLOCK, N_META), F32), win_bias], -1)
        k_blk = jnp.concatenate([km, k_win], 1)
        v_blk = jnp.concatenate([vm, v_win], 1)
        return attend(q_blk, k_blk, v_blk, bias)

    o_real = lax.map(block, jnp.arange(n_blk))
    o_real = jnp.moveaxis(o_real, 0, 1).reshape(Bn, S, Hk, G, dh)
    o = jnp.concatenate([o_meta, o_real], 1).reshape(Bn, L, C_WIDTH)
    return o * jax.nn.silu(z)


def setup_inputs(seed: int = 0) -> dict:
    key = jax.random.key(seed)
    ks = jax.random.split(key, 16)
    nrm = jax.random.normal
    d_in = D_MODEL ** -0.5
    x = nrm(ks[0], (BATCH, SEQ, D_MODEL), F32)
    meta_tokens = nrm(ks[1], (N_META, D_MODEL), F32)
    w_in_a = nrm(ks[2], (N_A, D_MODEL, 4 * A_WIDTH), F32) * d_in
    rpb_a = nrm(ks[3], (N_A, A_HEADS, 2 * NA_WIN_ROWS - 1, 2 * NA_WIN_COLS - 1), F32) * 0.1
    w_in_b = nrm(ks[4], (N_B, D_MODEL, 4 * B_WIDTH), F32) * d_in
    lam_q1_b = nrm(ks[5], (N_B, B_HEAD_DIM), F32) * 0.1
    lam_k1_b = nrm(ks[6], (N_B, B_HEAD_DIM), F32) * 0.1
    lam_q2_b = nrm(ks[7], (N_B, B_HEAD_DIM), F32) * 0.1
    lam_k2_b = nrm(ks[8], (N_B, B_HEAD_DIM), F32) * 0.1
    subln_g_b = 1.0 + 0.02 * nrm(ks[9], (N_B, 2 * B_HEAD_DIM), F32)
    w_in_c = nrm(ks[10], (N_C, D_MODEL, 2 * C_WIDTH + 2 * C_KV_HEADS * C_HEAD_DIM), F32) * d_in
    sink_c = nrm(ks[11], (N_C, C_HEADS), F32)
    w_out = nrm(ks[12], (DEPTH, D_MODEL, D_MODEL), F32) * (D_MODEL ** -0.5) * DEEPNORM_BETA
    ln_g = 1.0 + 0.02 * nrm(ks[13], (DEPTH, D_MODEL), F32)
    ln_b = 0.02 * nrm(ks[14], (DEPTH, D_MODEL), F32)
    return {'x': x, 'meta_tokens': meta_tokens, 'w_in_a': w_in_a, 'rpb_a': rpb_a,
            'w_in_b': w_in_b, 'lam_q1_b': lam_q1_b, 'lam_k1_b': lam_k1_b,
            'lam_q2_b': lam_q2_b, 'lam_k2_b': lam_k2_b, 'subln_g_b': subln_g_b,
            'w_in_c': w_in_c, 'sink_c': sink_c, 'w_out': w_out, 'ln_g': ln_g, 'ln_b': ln_b}


def reference(x, meta_tokens, w_in_a, rpb_a, w_in_b, lam_q1_b, lam_k1_b, lam_q2_b,
              lam_k2_b, subln_g_b, w_in_c, sink_c, w_out, ln_g, ln_b):
    Bn = x.shape[0]
    meta = jnp.broadcast_to(meta_tokens.astype(x.dtype)[None], (Bn, N_META, D_MODEL))
    h = jnp.concatenate([meta, x], axis=1)
    for i in range(DEPTH):
        kind, j = i % N_MIXERS, i // N_MIXERS
        if kind == 0:
            y = _na_branch(h, w_in_a[j], rpb_a[j])
        elif kind == 1:
            y = _diff_branch(h, w_in_b[j], lam_q1_b[j], lam_k1_b[j], lam_q2_b[j],
                             lam_k2_b[j], subln_g_b[j], i)
        else:
            y = _swa_branch(h, w_in_c[j], sink_c[j])
        out = jnp.einsum('ble,ed->bld', y, w_out[i])
        h = _layer_norm(DEEPNORM_ALPHA * h + out, ln_g[i], ln_b[i])
    return h[:, N_META:]
```

```python
import functools
import math

import jax
import jax.numpy as jnp
import numpy as np
from jax import lax
from jax.experimental import pallas as pl
from jax.experimental.pallas import tpu as pltpu

F32 = jnp.float32
BF16 = jnp.bfloat16

D_MODEL = 2048
SEQ = 2048
DEPTH = 4
N_META = 16
GRID_W = 64
GRID_ROWS = SEQ // GRID_W
NA_ROWS = 8
NA_COLS = 16
HEAD_DIM = 128
A_HEADS = 16
B_HEADS = 8
C_HEADS = 16
C_KV_HEADS = 4
C_GROUP = C_HEADS // C_KV_HEADS
C_WINDOW = 128
ALPHA = (2 * DEPTH) ** 0.25
LN_EPS = 1e-5
RMS_EPS = 1e-5
NEG_INF = -1e30

LANES = 128
META_PAD = LANES
VMEM_LIMIT = 56 * 1024 * 1024

PROJ_TM = 1024
PROJ_TN = 512
LN_TM = 256
Q_TILE = 256
NA_GROUP_ROWS = Q_TILE // GRID_W
NA_WIN_ROWS = NA_GROUP_ROWS + NA_ROWS
NA_WIN = NA_WIN_ROWS * GRID_W
NA_PAD = (NA_ROWS // 2) * GRID_W
SWA_WIN = Q_TILE + 2 * C_WINDOW
DIFF_CHUNK = 512


def _dot_nt(a, b):
    return lax.dot_general(a, b, (((1,), (1,)), ((), ())), preferred_element_type=F32)


def _dot(a, b):
    return jnp.dot(a, b, preferred_element_type=F32)


def _silu(z):
    return z / (1.0 + jnp.exp(-z))


def _meta_lane_mask():
    lane = lax.broadcasted_iota(jnp.int32, (1, META_PAD), 1)
    return jnp.where(lane < N_META, 0.0, NEG_INF).astype(F32)


def _compiler_params(n_axes):
    return pltpu.CompilerParams(
        dimension_semantics=("arbitrary",) * n_axes, vmem_limit_bytes=VMEM_LIMIT)


def _inproj_kernel(x_ref, xm_ref, w_ref, o_ref, om_ref, wbf_ref, *, n_q_tiles, q_scale):
    j = pl.program_id(0)
    i = pl.program_id(1)
    scale = jnp.where(j < n_q_tiles, q_scale, 1.0).astype(F32)

    @pl.when(i == 0)
    def _():
        wbf_ref[...] = w_ref[...].astype(BF16)
        om_ref[...] = (_dot(xm_ref[...], wbf_ref[...]) * scale).astype(BF16)

    o_ref[...] = (_dot(x_ref[...], wbf_ref[...]) * scale).astype(BF16)


def _inproj(x_bf, xm_bf, w, q_cols):
    m, d = x_bf.shape
    mm = xm_bf.shape[0]
    n = w.shape[1]
    tm = min(PROJ_TM, m)
    assert m % tm == 0 and n % PROJ_TN == 0 and q_cols % PROJ_TN == 0
    kern = functools.partial(_inproj_kernel, n_q_tiles=q_cols // PROJ_TN,
                             q_scale=HEAD_DIM ** -0.5)
    return pl.pallas_call(
        kern,
        out_shape=(jax.ShapeDtypeStruct((m, n), BF16), jax.ShapeDtypeStruct((mm, n), BF16)),
        grid=(n // PROJ_TN, m // tm),
        in_specs=[
            pl.BlockSpec((tm, d), lambda j, i: (i, 0)),
            pl.BlockSpec((mm, d), lambda j, i: (0, 0)),
            pl.BlockSpec((d, PROJ_TN), lambda j, i: (0, j)),
        ],
        out_specs=(
            pl.BlockSpec((tm, PROJ_TN), lambda j, i: (i, j)),
            pl.BlockSpec((mm, PROJ_TN), lambda j, i: (0, j)),
        ),
        scratch_shapes=[pltpu.VMEM((d, PROJ_TN), BF16)],
        compiler_params=_compiler_params(2),
        name="inproj",
    )(x_bf, xm_bf, w)


def _outproj_ln_kernel(y_ref, ym_ref, w_ref, h_ref, hm_ref, g_ref, b_ref, *out_refs, emit_bf16):
    def layer_norm(y, h):
        t = ALPHA * h + _dot(y, w_ref[...])
        mu = jnp.mean(t, -1, keepdims=True)
        d = t - mu
        var = jnp.mean(d * d, -1, keepdims=True)
        return d * lax.rsqrt(var + LN_EPS) * g_ref[...] + b_ref[...]

    if emit_bf16:
        o_ref, om_ref, obf_ref, ombf_ref = out_refs
    else:
        o_ref, om_ref = out_refs

    @pl.when(pl.program_id(0) == 0)
    def _():
        r = layer_norm(ym_ref[...], hm_ref[...])
        om_ref[...] = r
        if emit_bf16:
            ombf_ref[...] = r.astype(BF16)

    r = layer_norm(y_ref[...], h_ref[...])
    o_ref[...] = r
    if emit_bf16:
        obf_ref[...] = r.astype(BF16)


def _outproj_ln(y, ym, w_bf, h, hm, g, b, emit_bf16):
    m, d = h.shape
    mm = hm.shape[0]
    assert m % LN_TM == 0
    row = pl.BlockSpec((LN_TM, d), lambda i: (i, 0))
    meta = pl.BlockSpec((mm, d), lambda i: (0, 0))
    vec = pl.BlockSpec((1, d), lambda i: (0, 0))
    out_shape = [jax.ShapeDtypeStruct((m, d), F32), jax.ShapeDtypeStruct((mm, d), F32)]
    out_specs = [row, meta]
    if emit_bf16:
        out_shape += [jax.ShapeDtypeStruct((m, d), BF16), jax.ShapeDtypeStruct((mm, d), BF16)]
        out_specs += [row, meta]
    return pl.pallas_call(
        functools.partial(_outproj_ln_kernel, emit_bf16=emit_bf16),
        out_shape=tuple(out_shape),
        grid=(m // LN_TM,),
        in_specs=[row, meta, pl.BlockSpec((d, d), lambda i: (0, 0)), row, meta, vec, vec],
        out_specs=tuple(out_specs),
        compiler_params=_compiler_params(1),
        name="outproj_ln",
    )(y, ym, w_bf, h, hm, g.reshape(1, d), b.reshape(1, d))


def _na_tables():
    qa, qc = np.arange(Q_TILE) // GRID_W, np.arange(Q_TILE) % GRID_W
    ke, kc = np.arange(NA_WIN) // GRID_W, np.arange(NA_WIN) % GRID_W
    dr = ke[None, :] - NA_ROWS // 2 - qa[:, None]
    dc = kc[None, :] - qc[:, None]
    c0 = np.clip(qc - NA_COLS // 2, 0, GRID_W - NA_COLS)
    col_ok = (kc[None, :] >= c0[:, None]) & (kc[None, :] < c0[:, None] + NA_COLS)
    assert dr.min() >= -(NA_ROWS - 1) and dr.max() <= NA_ROWS - 1
    assert np.abs(dc[col_ok]).max() <= NA_COLS - 1
    dr_idx = dr + NA_ROWS - 1
    dc_idx = np.clip(dc + NA_COLS - 1, 0, 2 * NA_COLS - 2)

    n_groups = GRID_ROWS // NA_GROUP_ROWS
    masks = []
    for g in range(n_groups):
        r = NA_GROUP_ROWS * g + qa
        r0 = np.clip(r - NA_ROWS // 2, 0, GRID_ROWS - NA_ROWS)
        kr = NA_GROUP_ROWS * g - NA_ROWS // 2 + ke
        ok = (kr[None, :] >= r0[:, None]) & (kr[None, :] < r0[:, None] + NA_ROWS)
        masks.append(np.where(ok, 0.0, NEG_INF).astype(np.float32))
    for g in range(1, n_groups - 1):
        assert np.array_equal(masks[g], masks[1])
    row_mask = np.stack([masks[0], masks[1], masks[n_groups - 1]])
    return dr_idx, dc_idx, col_ok, row_mask


def _na_kernel(q_ref, k_ref, v_ref, z_ref, qm_ref, km_ref, vm_ref, zm_ref, bias_ref, rmask_ref,
               y_ref, ym_ref, kpad, vpad, kmp, vmp, comb, *, n_groups):
    b = pl.program_id(1)
    g = pl.program_id(2)
    meta_mask = _meta_lane_mask()

    @pl.when(jnp.logical_and(b == 0, g == 0))
    def _():
        for c in range(3):
            comb[c] = bias_ref[...] + rmask_ref[c]

    @pl.when(g == 0)
    def _():
        zeros = jnp.zeros((NA_PAD, HEAD_DIM), BF16)
        for pad, src in ((kpad, k_ref), (vpad, v_ref)):
            pad[0:NA_PAD] = zeros
            pad[NA_PAD:NA_PAD + SEQ] = src[...]
            pad[NA_PAD + SEQ:NA_PAD + SEQ + NA_PAD] = zeros
        for pad, src in ((kmp, km_ref), (vmp, vm_ref)):
            pad[...] = jnp.zeros((META_PAD, HEAD_DIM), BF16)
            pad[0:N_META] = src[...]
        s = _dot_nt(qm_ref[...], kmp[...]) + meta_mask
        p = jnp.exp(s - jnp.max(s, -1, keepdims=True))
        o = _dot(p.astype(BF16), vmp[...]) / jnp.sum(p, -1, keepdims=True)
        ym_ref[...] = (o * _silu(zm_ref[...].astype(F32))).astype(BF16)

    start = pl.multiple_of(g * Q_TILE, Q_TILE)
    kw = kpad[pl.ds(start, NA_WIN), :]
    vw = vpad[pl.ds(start, NA_WIN), :]
    kind = jnp.where(g == 0, 0, jnp.where(g == n_groups - 1, 2, 1))
    q = q_ref[...]
    s_w = _dot_nt(q, kw) + comb[kind]
    s_m = _dot_nt(q, kmp[...]) + meta_mask
    m = jnp.maximum(jnp.max(s_w, -1, keepdims=True), jnp.max(s_m, -1, keepdims=True))
    p_w = jnp.exp(s_w - m)
    p_m = jnp.exp(s_m - m)
    l = jnp.sum(p_w, -1, keepdims=True) + jnp.sum(p_m, -1, keepdims=True)
    o = (_dot(p_w.astype(BF16), vw) + _dot(p_m.astype(BF16), vmp[...])) / l
    y_ref[...] = (o * _silu(z_ref[...].astype(F32))).astype(BF16)


def _na_attention(qkvz, qkvz_m, rpb, bn):
    width = A_HEADS * HEAD_DIM
    n_groups = SEQ // Q_TILE
    dr_idx, dc_idx, col_ok, row_mask = _na_tables()
    bias = jnp.where(col_ok[None], rpb[:, dr_idx, dc_idx], NEG_INF).astype(F32)
    hb = width // HEAD_DIM

    def real(sec):
        return pl.BlockSpec((Q_TILE, HEAD_DIM), lambda h, b, g: (b * n_groups + g, sec * hb + h))

    def full(sec):
        return pl.BlockSpec((SEQ, HEAD_DIM), lambda h, b, g: (b, sec * hb + h))

    def meta(sec):
        return pl.BlockSpec((N_META, HEAD_DIM), lambda h, b, g: (b, sec * hb + h))

    return pl.pallas_call(
        functools.partial(_na_kernel, n_groups=n_groups),
        out_shape=(jax.ShapeDtypeStruct((bn * SEQ, width), BF16),
                   jax.ShapeDtypeStruct((bn * N_META, width), BF16)),
        grid=(A_HEADS, bn, n_groups),
        in_specs=[
            real(0), full(1), full(2), real(3), meta(0), meta(1), meta(2), meta(3),
            pl.BlockSpec((None, Q_TILE, NA_WIN), lambda h, b, g: (h, 0, 0)),
            pl.BlockSpec((3, Q_TILE, NA_WIN), lambda h, b, g: (0, 0, 0)),
        ],
        out_specs=(
            pl.BlockSpec((Q_TILE, HEAD_DIM), lambda h, b, g: (b * n_groups + g, h)),
            pl.BlockSpec((N_META, HEAD_DIM), lambda h, b, g: (b, h)),
        ),
        scratch_shapes=[
            pltpu.VMEM((SEQ + 2 * NA_PAD, HEAD_DIM), BF16),
            pltpu.VMEM((SEQ + 2 * NA_PAD, HEAD_DIM), BF16),
            pltpu.VMEM((META_PAD, HEAD_DIM), BF16),
            pltpu.VMEM((META_PAD, HEAD_DIM), BF16),
            pltpu.VMEM((3, Q_TILE, NA_WIN), F32),
        ],
        compiler_params=_compiler_params(3),
        name="na_attention",
    )(qkvz, qkvz, qkvz, qkvz, qkvz_m, qkvz_m, qkvz_m, qkvz_m, bias, jnp.asarray(row_mask))


def _diff_kernel(slopes_ref, q_ref, k_ref, v_ref, z_ref, qm_ref, km_ref, vm_ref, zm_ref,
                 lq1_ref, lk1_ref, lq2_ref, lk2_ref, subg_ref, y_ref, ym_ref,
                 kmp, vmp, tbl, sbuf, *, lambda_init):
    h = pl.program_id(0)
    b = pl.program_id(1)
    n = pl.program_id(2)
    n_real = SEQ // LANES
    n_chunk = SEQ // DIFF_CHUNK
    per_chunk = DIFF_CHUNK // LANES
    tbl_off = SEQ - Q_TILE
    meta_mask = _meta_lane_mask()
    lam = (jnp.exp(jnp.sum(lq1_ref[...] * lk1_ref[...], -1, keepdims=True))
           - jnp.exp(jnp.sum(lq2_ref[...] * lk2_ref[...], -1, keepdims=True)) + lambda_init)

    def finish(o, z):
        o = o * lax.rsqrt(jnp.mean(o * o, -1, keepdims=True) + RMS_EPS)
        o = o * subg_ref[...] * (1.0 - lambda_init)
        return (o * _silu(z.astype(F32))).astype(BF16)

    @pl.when(jnp.logical_and(b == 0, n == 0))
    def _():
        slope = slopes_ref[h]
        qi = lax.broadcasted_iota(jnp.int32, (Q_TILE, LANES), 0)
        kl = lax.broadcasted_iota(jnp.int32, (Q_TILE, LANES), 1)
        for t in range(tbl.shape[0]):
            tbl[t] = -slope * jnp.abs(qi - kl - (LANES * t - tbl_off)).astype(F32)

    @pl.when(n == 0)
    def _():
        for pad, src in ((kmp, km_ref), (vmp, vm_ref)):
            pad[...] = jnp.zeros((META_PAD, 2 * HEAD_DIM), BF16)
            pad[0:N_META] = src[...]
        qm = qm_ref[...]
        probs = []
        for j in range(2):
            cols = slice(j * HEAD_DIM, (j + 1) * HEAD_DIM)
            s_r = _dot_nt(qm[:, cols], k_ref[:, cols])
            s_m = _dot_nt(qm[:, cols], kmp[:, cols]) + meta_mask
            m = jnp.maximum(jnp.max(s_r, -1, keepdims=True), jnp.max(s_m, -1, keepdims=True))
            p_r = jnp.exp(s_r - m)
            p_m = jnp.exp(s_m - m)
            l = jnp.sum(p_r, -1, keepdims=True) + jnp.sum(p_m, -1, keepdims=True)
            probs.append((p_r / l, p_m / l))
        a_r = probs[0][0] - lam * probs[1][0]
        a_m = probs[0][1] - lam * probs[1][1]
        o = _dot(a_r.astype(BF16), v_ref[...]) + _dot(a_m.astype(BF16), vmp[...])
        ym_ref[...] = finish(o, zm_ref[...])

    q = q_ref[...]
    t0 = tbl_off // LANES - n * (Q_TILE // LANES)
    ls = []
    for j in range(2):
        cols = slice(j * HEAD_DIM, (j + 1) * HEAD_DIM)
        qj = q[:, cols]
        mx = None
        for c in range(n_chunk):
            s = _dot_nt(qj, k_ref[c * DIFF_CHUNK:(c + 1) * DIFF_CHUNK, cols])
            for u in range(per_chunk):
                kb = c * per_chunk + u
                blk = s[:, u * LANES:(u + 1) * LANES] + tbl[t0 + kb]
                sbuf[j, kb] = blk
                mx = blk if mx is None else jnp.maximum(mx, blk)
        blk = _dot_nt(qj, kmp[:, cols]) + meta_mask
        sbuf[j, n_real] = blk
        mx = jnp.maximum(mx, blk)
        m = jnp.max(mx, -1, keepdims=True)
        acc = None
        for kb in range(n_real + 1):
            p = jnp.exp(sbuf[j, kb] - m)
            sbuf[j, kb] = p
            acc = p if acc is None else acc + p
        ls.append(jnp.sum(acc, -1, keepdims=True))
    r = lam * ls[0] / ls[1]
    o = None
    for c in range(n_chunk):
        a = jnp.concatenate(
            [sbuf[0, c * per_chunk + u] - r * sbuf[1, c * per_chunk + u] for u in range(per_chunk)],
            axis=1)
        part = _dot(a.astype(BF16), v_ref[c * DIFF_CHUNK:(c + 1) * DIFF_CHUNK, :])
        o = part if o is None else o + part
    a = sbuf[0, n_real] - r * sbuf[1, n_real]
    o = (o + _dot(a.astype(BF16), vmp[...])) / ls[0]
    y_ref[...] = finish(o, z_ref[...])


def _alibi_slopes(n_heads):
    return jnp.exp2(-8.0 * jnp.arange(1, n_heads + 1, dtype=F32) / n_heads)


def _diff_attention(qkvz, qkvz_m, lq1, lk1, lq2, lk2, subg, layer_idx, bn):
    width = 2 * B_HEADS * HEAD_DIM
    hw = 2 * HEAD_DIM
    n_blocks = SEQ // Q_TILE
    lambda_init = 0.8 - 0.6 * math.exp(-0.3 * layer_idx)

    def real(sec):
        return pl.BlockSpec((Q_TILE, hw), lambda h, b, n: (b * n_blocks + n, sec * B_HEADS + h))

    def full(sec):
        return pl.BlockSpec((SEQ, hw), lambda h, b, n: (b, sec * B_HEADS + h))

    def meta(sec):
        return pl.BlockSpec((N_META, hw), lambda h, b, n: (b, sec * B_HEADS + h))

    def vec(width_):
        return pl.BlockSpec((1, width_), lambda h, b, n: (0, 0))

    n_tbl = (SEQ - Q_TILE + SEQ) // LANES
    return pl.pallas_call(
        functools.partial(_diff_kernel, lambda_init=lambda_init),
        out_shape=(jax.ShapeDtypeStruct((bn * SEQ, width), BF16),
                   jax.ShapeDtypeStruct((bn * N_META, width), BF16)),
        grid=(B_HEADS, bn, n_blocks),
        in_specs=[
            pl.BlockSpec(memory_space=pltpu.SMEM),
            real(0), full(1), full(2), real(3), meta(0), meta(1), meta(2), meta(3),
            vec(HEAD_DIM), vec(HEAD_DIM), vec(HEAD_DIM), vec(HEAD_DIM), vec(hw),
        ],
        out_specs=(
            pl.BlockSpec((Q_TILE, hw), lambda h, b, n: (b * n_blocks + n, h)),
            pl.BlockSpec((N_META, hw), lambda h, b, n: (b, h)),
        ),
        scratch_shapes=[
            pltpu.VMEM((META_PAD, hw), BF16),
            pltpu.VMEM((META_PAD, hw), BF16),
            pltpu.VMEM((n_tbl, Q_TILE, LANES), F32),
            pltpu.VMEM((2, SEQ // LANES + 1, Q_TILE, LANES), F32),
        ],
        compiler_params=_compiler_params(3),
        name="diff_attention",
    )(_alibi_slopes(B_HEADS), qkvz, qkvz, qkvz, qkvz, qkvz_m, qkvz_m, qkvz_m, qkvz_m,
      lq1.reshape(1, -1), lk1.reshape(1, -1), lq2.reshape(1, -1), lk2.reshape(1, -1),
      subg.reshape(1, -1))


def _swa_tables():
    qi = np.arange(Q_TILE)[:, None]
    kj = np.arange(SWA_WIN)[None, :]
    dist = np.abs(qi + C_WINDOW - kj)
    near = dist <= C_WINDOW
    valid = np.stack([near & (kj >= C_WINDOW), near, near & (kj < C_WINDOW + Q_TILE)])
    return dist.astype(np.float32), valid.astype(np.float32)


def _swa_kernel(slopes_ref, sink_ref, q_ref, k_ref, v_ref, z_ref, qm_ref, km_ref, vm_ref, zm_ref,
                dist_ref, valid_ref, y_ref, ym_ref, kpad, vpad, kmp, vmp, comb, *, n_blocks):
    kh = pl.program_id(0)
    b = pl.program_id(1)
    n = pl.program_id(2)
    meta_mask = _meta_lane_mask()

    def stack_heads(x):
        return jnp.concatenate(
            [x[:, gq * HEAD_DIM:(gq + 1) * HEAD_DIM] for gq in range(C_GROUP)], axis=0)

    def sink_column(rows):
        return jnp.concatenate(
            [jnp.full((rows, 1), sink_ref[kh * C_GROUP + gq], F32) for gq in range(C_GROUP)], axis=0)

    def gate_and_unstack(o, z, rows):
        z = z.astype(F32)
        return jnp.concatenate(
            [o[gq * rows:(gq + 1) * rows] * _silu(z[:, gq * HEAD_DIM:(gq + 1) * HEAD_DIM])
             for gq in range(C_GROUP)], axis=1).astype(BF16)

    @pl.when(jnp.logical_and(b == 0, n == 0))
    def _():
        for c in range(3):
            for gq in range(C_GROUP):
                slope = slopes_ref[kh * C_GROUP + gq]
                comb[c, gq * Q_TILE:(gq + 1) * Q_TILE, :] = jnp.where(
                    valid_ref[c] > 0.5, -slope * dist_ref[...], NEG_INF)

    @pl.when(n == 0)
    def _():
        zeros = jnp.zeros((C_WINDOW, HEAD_DIM), BF16)
        for pad, src in ((kpad, k_ref), (vpad, v_ref)):
            pad[0:C_WINDOW] = zeros
            pad[C_WINDOW:C_WINDOW + SEQ] = src[...]
            pad[C_WINDOW + SEQ:SEQ + 2 * C_WINDOW] = zeros
        for pad, src in ((kmp, km_ref), (vmp, vm_ref)):
            pad[...] = jnp.zeros((META_PAD, HEAD_DIM), BF16)
            pad[0:N_META] = src[...]
        s = _dot_nt(stack_heads(qm_ref[...]), kmp[...]) + meta_mask
        sk = sink_column(N_META)
        m = jnp.maximum(jnp.max(s, -1, keepdims=True), sk)
        p = jnp.exp(s - m)
        l = jnp.sum(p, -1, keepdims=True) + jnp.exp(sk - m)
        o = _dot(p.astype(BF16), vmp[...]) / l
        ym_ref[...] = gate_and_unstack(o, zm_ref[...], N_META)

    start = pl.multiple_of(n * Q_TILE, Q_TILE)
    kw = kpad[pl.ds(start, SWA_WIN), :]
    vw = vpad[pl.ds(start, SWA_WIN), :]
    kind = jnp.where(n == 0, 0, jnp.where(n == n_blocks - 1, 2, 1))
    q = stack_heads(q_ref[...])
    s_w = _dot_nt(q, kw) + comb[kind]
    s_m = _dot_nt(q, kmp[...]) + meta_mask
    sk = sink_column(Q_TILE)
    m = jnp.maximum(jnp.maximum(jnp.max(s_w, -1, keepdims=True), jnp.max(s_m, -1, keepdims=True)), sk)
    p_w = jnp.exp(s_w - m)
    p_m = jnp.exp(s_m - m)
    l = jnp.sum(p_w, -1, keepdims=True) + jnp.sum(p_m, -1, keepdims=True) + jnp.exp(sk - m)
    o = (_dot(p_w.astype(BF16), vw) + _dot(p_m.astype(BF16), vmp[...])) / l
    y_ref[...] = gate_and_unstack(o, z_ref[...], Q_TILE)


def _swa_attention(qkvz, qkvz_m, sink, bn):
    width = C_HEADS * HEAD_DIM
    gw = C_GROUP * HEAD_DIM
    n_blocks = SEQ // Q_TILE
    assert n_blocks >= 2
    k_blk0 = width // HEAD_DIM
    v_blk0 = k_blk0 + C_KV_HEADS
    z_blk0 = (width + 2 * C_KV_HEADS * HEAD_DIM) // gw
    dist, valid = _swa_tables()

    def q_like(blk0):
        return pl.BlockSpec((Q_TILE, gw), lambda kh, b, n: (b * n_blocks + n, blk0 + kh))

    def kv(blk0):
        return pl.BlockSpec((SEQ, HEAD_DIM), lambda kh, b, n: (b, blk0 + kh))

    def meta(cols, blk0):
        return pl.BlockSpec((N_META, cols), lambda kh, b, n: (b, blk0 + kh))

    smem = pl.BlockSpec(memory_space=pltpu.SMEM)
    return pl.pallas_call(
        functools.partial(_swa_kernel, n_blocks=n_blocks),
        out_shape=(jax.ShapeDtypeStruct((bn * SEQ, width), BF16),
                   jax.ShapeDtypeStruct((bn * N_META, width), BF16)),
        grid=(C_KV_HEADS, bn, n_blocks),
        in_specs=[
            smem, smem,
            q_like(0), kv(k_blk0), kv(v_blk0), q_like(z_blk0),
            meta(gw, 0), meta(HEAD_DIM, k_blk0), meta(HEAD_DIM, v_blk0), meta(gw, z_blk0),
            pl.BlockSpec((Q_TILE, SWA_WIN), lambda kh, b, n: (0, 0)),
            pl.BlockSpec((3, Q_TILE, SWA_WIN), lambda kh, b, n: (0, 0, 0)),
        ],
        out_specs=(
            pl.BlockSpec((Q_TILE, gw), lambda kh, b, n: (b * n_blocks + n, kh)),
            pl.BlockSpec((N_META, gw), lambda kh, b, n: (b, kh)),
        ),
        scratch_shapes=[
            pltpu.VMEM((SEQ + 2 * C_WINDOW, HEAD_DIM), BF16),
            pltpu.VMEM((SEQ + 2 * C_WINDOW, HEAD_DIM), BF16),
            pltpu.VMEM((META_PAD, HEAD_DIM), BF16),
            pltpu.VMEM((META_PAD, HEAD_DIM), BF16),
            pltpu.VMEM((3, C_GROUP * Q_TILE, SWA_WIN), F32),
        ],
        compiler_params=_compiler_params(3),
        name="swa_attention",
    )(_alibi_slopes(C_HEADS), sink.astype(F32), qkvz, qkvz, qkvz, qkvz,
      qkvz_m, qkvz_m, qkvz_m, qkvz_m, jnp.asarray(dist), jnp.asarray(valid))


def kernel(x, meta_tokens, w_in_a, rpb_a, w_in_b, lam_q1_b, lam_k1_b, lam_q2_b, lam_k2_b,
           subln_g_b, w_in_c, sink_c, w_out, ln_g, ln_b):
    bn, seq, d = x.shape
    assert seq == SEQ and d == D_MODEL
    h = x.reshape(bn * seq, d)
    hm = jnp.tile(meta_tokens.astype(F32), (bn, 1))
    h_bf, hm_bf = h.astype(BF16), hm.astype(BF16)
    for i in range(DEPTH):
        kind, j = i % 3, i // 3
        if kind == 0:
            qkvz, qkvz_m = _inproj(h_bf, hm_bf, w_in_a[j], A_HEADS * HEAD_DIM)
            y, ym = _na_attention(qkvz, qkvz_m, rpb_a[j], bn)
        elif kind == 1:
            qkvz, qkvz_m = _inproj(h_bf, hm_bf, w_in_b[j], 2 * B_HEADS * HEAD_DIM)
            y, ym = _diff_attention(qkvz, qkvz_m, lam_q1_b[j], lam_k1_b[j], lam_q2_b[j],
                                    lam_k2_b[j], subln_g_b[j], i, bn)
        else:
            qkvz, qkvz_m = _inproj(h_bf, hm_bf, w_in_c[j], C_HEADS * HEAD_DIM)
            y, ym = _swa_attention(qkvz, qkvz_m, sink_c[j], bn)
        last = i == DEPTH - 1
        outs = _outproj_ln(y, ym, w_out[i].astype(BF16), h, hm, ln_g[i], ln_b[i], not last)
        if last:
            h, hm = outs
        else:
            h, hm, h_bf, hm_bf = outs
    return h.reshape(bn, seq, d)
```

```python
import functools
import math

import jax
import jax.numpy as jnp
import numpy as np
from jax import lax
from jax.experimental import pallas as pl
from jax.experimental.pallas import tpu as pltpu

F32 = jnp.float32
BF16 = jnp.bfloat16

D_MODEL = 2048
SEQ = 2048
DEPTH = 4
N_META = 16
GRID_W = 64
GRID_ROWS = SEQ // GRID_W
NA_ROWS = 8
NA_COLS = 16
HEAD_DIM = 128
A_HEADS = 16
B_HEADS = 8
C_HEADS = 16
C_KV_HEADS = 4
C_GROUP = C_HEADS // C_KV_HEADS
C_WINDOW = 128
ALPHA = (2 * DEPTH) ** 0.25
LN_EPS = 1e-5
RMS_EPS = 1e-5
NEG_INF = -1e30

LANES = 128
META_PAD = LANES
VMEM_LIMIT = 56 * 1024 * 1024

PROJ_TM = 1024
PROJ_TN = 512
LN_TM = 256
Q_TILE = 256
NA_GROUP_ROWS = Q_TILE // GRID_W
NA_WIN_ROWS = NA_GROUP_ROWS + NA_ROWS
NA_WIN = NA_WIN_ROWS * GRID_W
NA_PAD = (NA_ROWS // 2) * GRID_W
SWA_WIN = Q_TILE + 2 * C_WINDOW
DIFF_CHUNK = 512


def _dot_nt(a, b):
    return lax.dot_general(a, b, (((1,), (1,)), ((), ())), preferred_element_type=F32)


def _dot(a, b):
    return jnp.dot(a, b, preferred_element_type=F32)


def _silu(z):
    return z / (1.0 + jnp.exp(-z))


def _meta_lane_mask():
    lane = lax.broadcasted_iota(jnp.int32, (1, META_PAD), 1)
    return jnp.where(lane < N_META, 0.0, NEG_INF).astype(F32)


def _compiler_params(n_axes):
    return pltpu.CompilerParams(
        dimension_semantics=("arbitrary",) * n_axes, vmem_limit_bytes=VMEM_LIMIT)


def _inproj_kernel(x_ref, xm_ref, w_ref, o_ref, om_ref, wbf_ref, *, n_q_tiles, q_scale):
    j = pl.program_id(0)
    i = pl.program_id(1)
    scale = jnp.where(j < n_q_tiles, q_scale, 1.0).astype(F32)

    @pl.when(i == 0)
    def _():
        wbf_ref[...] = w_ref[...].astype(BF16)
        om_ref[...] = (_dot(xm_ref[...], wbf_ref[...]) * scale).astype(BF16)

    o_ref[...] = (_dot(x_ref[...], wbf_ref[...]) * scale).astype(BF16)


def _inproj(x_bf, xm_bf, w_stack, layer, q_cols):
    m, d = x_bf.shape
    mm = xm_bf.shape[0]
    n = w_stack.shape[2]
    tm = min(PROJ_TM, m)
    assert m % tm == 0 and n % PROJ_TN == 0 and q_cols % PROJ_TN == 0
    kern = functools.partial(_inproj_kernel, n_q_tiles=q_cols // PROJ_TN,
                             q_scale=HEAD_DIM ** -0.5)
    return pl.pallas_call(
        kern,
        out_shape=(jax.ShapeDtypeStruct((m, n), BF16), jax.ShapeDtypeStruct((mm, n), BF16)),
        grid=(n // PROJ_TN, m // tm),
        in_specs=[
            pl.BlockSpec((tm, d), lambda j, i: (i, 0)),
            pl.BlockSpec((mm, d), lambda j, i: (0, 0)),
            pl.BlockSpec((None, d, PROJ_TN), lambda j, i: (layer, 0, j)),
        ],
        out_specs=(
            pl.BlockSpec((tm, PROJ_TN), lambda j, i: (i, j)),
            pl.BlockSpec((mm, PROJ_TN), lambda j, i: (0, j)),
        ),
        scratch_shapes=[pltpu.VMEM((d, PROJ_TN), BF16)],
        compiler_params=_compiler_params(2),
        name="inproj",
    )(x_bf, xm_bf, w_stack)


def _outproj_ln_kernel(y_ref, ym_ref, w_ref, h_ref, hm_ref, g_ref, b_ref, *out_refs, emit_bf16):
    def layer_norm(y, h):
        t = ALPHA * h + _dot(y, w_ref[...])
        mu = jnp.mean(t, -1, keepdims=True)
        d = t - mu
        var = jnp.mean(d * d, -1, keepdims=True)
        return d * lax.rsqrt(var + LN_EPS) * g_ref[...] + b_ref[...]

    if emit_bf16:
        o_ref, om_ref, obf_ref, ombf_ref = out_refs
    else:
        o_ref, om_ref = out_refs

    @pl.when(pl.program_id(0) == 0)
    def _():
        r = layer_norm(ym_ref[...], hm_ref[...])
        om_ref[...] = r
        if emit_bf16:
            ombf_ref[...] = r.astype(BF16)

    r = layer_norm(y_ref[...], h_ref[...])
    o_ref[...] = r
    if emit_bf16:
        obf_ref[...] = r.astype(BF16)


def _outproj_ln(y, ym, w_stack_bf, layer, h, hm, g, b, emit_bf16):
    m, d = h.shape
    mm = hm.shape[0]
    assert m % LN_TM == 0
    row = pl.BlockSpec((LN_TM, d), lambda i: (i, 0))
    meta = pl.BlockSpec((mm, d), lambda i: (0, 0))
    vec = pl.BlockSpec((1, d), lambda i: (0, 0))
    out_shape = [jax.ShapeDtypeStruct((m, d), F32), jax.ShapeDtypeStruct((mm, d), F32)]
    out_specs = [row, meta]
    if emit_bf16:
        out_shape += [jax.ShapeDtypeStruct((m, d), BF16), jax.ShapeDtypeStruct((mm, d), BF16)]
        out_specs += [row, meta]
    return pl.pallas_call(
        functools.partial(_outproj_ln_kernel, emit_bf16=emit_bf16),
        out_shape=tuple(out_shape),
        grid=(m // LN_TM,),
        in_specs=[row, meta, pl.BlockSpec((None, d, d), lambda i: (layer, 0, 0)), row, meta, vec, vec],
        out_specs=tuple(out_specs),
        compiler_params=_compiler_params(1),
        name="outproj_ln",
    )(y, ym, w_stack_bf, h, hm, g.reshape(1, d), b.reshape(1, d))


def _na_tables():
    qa = np.arange(NA_GROUP_ROWS)
    ke = np.arange(NA_WIN_ROWS)
    dr_idx = ke[None, :] - NA_ROWS // 2 - qa[:, None] + NA_ROWS - 1
    n_groups = GRID_ROWS // NA_GROUP_ROWS
    oks = []
    for g in range(n_groups):
        r = NA_GROUP_ROWS * g + qa
        r0 = np.clip(r - NA_ROWS // 2, 0, GRID_ROWS - NA_ROWS)
        kr = NA_GROUP_ROWS * g - NA_ROWS // 2 + ke
        oks.append((kr[None, :] >= r0[:, None]) & (kr[None, :] < r0[:, None] + NA_ROWS))
    for g in range(1, n_groups - 1):
        assert np.array_equal(oks[g], oks[1])
    row_ok = np.stack([oks[0], oks[1], oks[n_groups - 1]])
    assert dr_idx[row_ok.any(0)].min() >= 0 and dr_idx[row_ok.any(0)].max() <= 2 * NA_ROWS - 2
    return dr_idx, row_ok


_NA_TABLES = _na_tables()


def _na_build_bias(rpb_ref, toep, comb):
    dr_idx, row_ok = _NA_TABLES
    qc = lax.broadcasted_iota(jnp.int32, (GRID_W, LANES), 0)
    lane = lax.broadcasted_iota(jnp.int32, (GRID_W, LANES), 1)
    kc = lane & (GRID_W - 1)
    c0 = jnp.clip(qc - NA_COLS // 2, 0, GRID_W - NA_COLS)
    col_ok = jnp.logical_and(kc >= c0, kc < c0 + NA_COLS)
    dc_idx = jnp.clip(kc - qc + NA_COLS - 1, 0, LANES - 1)
    for dr in range(2 * NA_ROWS - 1):
        row = jnp.broadcast_to(rpb_ref[dr:dr + 1, :], (GRID_W, LANES))
        toep[dr] = jnp.where(col_ok, jnp.take_along_axis(row, dc_idx, axis=1), NEG_INF)
    masked = jnp.full((GRID_W, LANES), NEG_INF, F32)
    left_half = lane < GRID_W
    for kind in range(3):
        for a in range(NA_GROUP_ROWS):
            for ep in range(NA_WIN_ROWS // 2):
                halves = [toep[int(dr_idx[a, e])] if row_ok[kind, a, e] else masked
                          for e in (2 * ep, 2 * ep + 1)]
                comb[kind, a * GRID_W:(a + 1) * GRID_W, ep * LANES:(ep + 1) * LANES] = jnp.where(
                    left_half, halves[0], halves[1])


def _na_kernel(q_ref, k_ref, v_ref, z_ref, qm_ref, km_ref, vm_ref, zm_ref, rpb_ref,
               y_ref, ym_ref, kpad, vpad, kmp, vmp, toep, comb, *, n_groups):
    b = pl.program_id(1)
    g = pl.program_id(2)
    meta_mask = _meta_lane_mask()

    @pl.when(jnp.logical_and(b == 0, g == 0))
    def _():
        _na_build_bias(rpb_ref, toep, comb)

    @pl.when(g == 0)
    def _():
        zeros = jnp.zeros((NA_PAD, HEAD_DIM), BF16)
        for pad, src in ((kpad, k_ref), (vpad, v_ref)):
            pad[0:NA_PAD] = zeros
            pad[NA_PAD:NA_PAD + SEQ] = src[...]
            pad[NA_PAD + SEQ:NA_PAD + SEQ + NA_PAD] = zeros
        for pad, src in ((kmp, km_ref), (vmp, vm_ref)):
            pad[...] = jnp.zeros((META_PAD, HEAD_DIM), BF16)
            pad[0:N_META] = src[...]
        s = _dot_nt(qm_ref[...], kmp[...]) + meta_mask
        p = jnp.exp(s - jnp.max(s, -1, keepdims=True))
        o = _dot(p.astype(BF16), vmp[...]) / jnp.sum(p, -1, keepdims=True)
        ym_ref[...] = (o * _silu(zm_ref[...].astype(F32))).astype(BF16)

    start = pl.multiple_of(g * Q_TILE, Q_TILE)
    kw = kpad[pl.ds(start, NA_WIN), :]
    vw = vpad[pl.ds(start, NA_WIN), :]
    kind = jnp.where(g == 0, 0, jnp.where(g == n_groups - 1, 2, 1))
    q = q_ref[...]
    s_w = _dot_nt(q, kw) + comb[kind]
    s_m = _dot_nt(q, kmp[...]) + meta_mask
    m = jnp.maximum(jnp.max(s_w, -1, keepdims=True), jnp.max(s_m, -1, keepdims=True))
    p_w = jnp.exp(s_w - m)
    p_m = jnp.exp(s_m - m)
    l = jnp.sum(p_w, -1, keepdims=True) + jnp.sum(p_m, -1, keepdims=True)
    o = (_dot(p_w.astype(BF16), vw) + _dot(p_m.astype(BF16), vmp[...])) / l
    y_ref[...] = (o * _silu(z_ref[...].astype(F32))).astype(BF16)


def _na_attention(qkvz, qkvz_m, rpb, bn):
    width = A_HEADS * HEAD_DIM
    n_groups = SEQ // Q_TILE
    n_dr, n_dc = rpb.shape[1:]
    rpb_pad = jnp.pad(rpb.astype(F32), ((0, 0), (0, 16 - n_dr), (0, LANES - n_dc)))
    hb = width // HEAD_DIM

    def real(sec):
        return pl.BlockSpec((Q_TILE, HEAD_DIM), lambda h, b, g: (b * n_groups + g, sec * hb + h))

    def full(sec):
        return pl.BlockSpec((SEQ, HEAD_DIM), lambda h, b, g: (b, sec * hb + h))

    def meta(sec):
        return pl.BlockSpec((N_META, HEAD_DIM), lambda h, b, g: (b, sec * hb + h))

    return pl.pallas_call(
        functools.partial(_na_kernel, n_groups=n_groups),
        out_shape=(jax.ShapeDtypeStruct((bn * SEQ, width), BF16),
                   jax.ShapeDtypeStruct((bn * N_META, width), BF16)),
        grid=(A_HEADS, bn, n_groups),
        in_specs=[
            real(0), full(1), full(2), real(3), meta(0), meta(1), meta(2), meta(3),
            pl.BlockSpec((None, 16, LANES), lambda h, b, g: (h, 0, 0)),
        ],
        out_specs=(
            pl.BlockSpec((Q_TILE, HEAD_DIM), lambda h, b, g: (b * n_groups + g, h)),
            pl.BlockSpec((N_META, HEAD_DIM), lambda h, b, g: (b, h)),
        ),
        scratch_shapes=[
            pltpu.VMEM((SEQ + 2 * NA_PAD, HEAD_DIM), BF16),
            pltpu.VMEM((SEQ + 2 * NA_PAD, HEAD_DIM), BF16),
            pltpu.VMEM((META_PAD, HEAD_DIM), BF16),
            pltpu.VMEM((META_PAD, HEAD_DIM), BF16),
            pltpu.VMEM((2 * NA_ROWS - 1, GRID_W, LANES), F32),
            pltpu.VMEM((3, Q_TILE, NA_WIN), F32),
        ],
        compiler_params=_compiler_params(3),
        name="na_attention",
    )(qkvz, qkvz, qkvz, qkvz, qkvz_m, qkvz_m, qkvz_m, qkvz_m, rpb_pad)


def _diff_kernel(slopes_ref, q_ref, k_ref, v_ref, z_ref, qm_ref, km_ref, vm_ref, zm_ref,
                 lq1_ref, lk1_ref, lq2_ref, lk2_ref, subg_ref, y_ref, ym_ref,
                 kmp, vmp, tbl, sbuf, *, lambda_init):
    h = pl.program_id(0)
    b = pl.program_id(1)
    n = pl.program_id(2)
    n_real = SEQ // LANES
    n_chunk = SEQ // DIFF_CHUNK
    per_chunk = DIFF_CHUNK // LANES
    tbl_off = SEQ - Q_TILE
    meta_mask = _meta_lane_mask()
    lam = (jnp.exp(jnp.sum(lq1_ref[...] * lk1_ref[...], -1, keepdims=True))
           - jnp.exp(jnp.sum(lq2_ref[...] * lk2_ref[...], -1, keepdims=True)) + lambda_init)

    def finish(o, z):
        o = o * lax.rsqrt(jnp.mean(o * o, -1, keepdims=True) + RMS_EPS)
        o = o * subg_ref[...] * (1.0 - lambda_init)
        return (o * _silu(z.astype(F32))).astype(BF16)

    @pl.when(jnp.logical_and(b == 0, n == 0))
    def _():
        slope = slopes_ref[h]
        qi = lax.broadcasted_iota(jnp.int32, (Q_TILE, LANES), 0)
        kl = lax.broadcasted_iota(jnp.int32, (Q_TILE, LANES), 1)
        for t in range(tbl.shape[0]):
            tbl[t] = -slope * jnp.abs(qi - kl - (LANES * t - tbl_off)).astype(F32)

    @pl.when(n == 0)
    def _():
        for pad, src in ((kmp, km_ref), (vmp, vm_ref)):
            pad[...] = jnp.zeros((META_PAD, 2 * HEAD_DIM), BF16)
            pad[0:N_META] = src[...]
        qm = qm_ref[...]
        probs = []
        for j in range(2):
            cols = slice(j * HEAD_DIM, (j + 1) * HEAD_DIM)
            s_r = _dot_nt(qm[:, cols], k_ref[:, cols])
            s_m = _dot_nt(qm[:, cols], kmp[:, cols]) + meta_mask
            m = jnp.maximum(jnp.max(s_r, -1, keepdims=True), jnp.max(s_m, -1, keepdims=True))
            p_r = jnp.exp(s_r - m)
            p_m = jnp.exp(s_m - m)
            l = jnp.sum(p_r, -1, keepdims=True) + jnp.sum(p_m, -1, keepdims=True)
            probs.append((p_r / l, p_m / l))
        a_r = probs[0][0] - lam * probs[1][0]
        a_m = probs[0][1] - lam * probs[1][1]
        o = _dot(a_r.astype(BF16), v_ref[...]) + _dot(a_m.astype(BF16), vmp[...])
        ym_ref[...] = finish(o, zm_ref[...])

    q = q_ref[...]
    t0 = tbl_off // LANES - n * (Q_TILE // LANES)
    ls = []
    for j in range(2):
        cols = slice(j * HEAD_DIM, (j + 1) * HEAD_DIM)
        qj = q[:, cols]
        mx = None
        for c in range(n_chunk):
            s = _dot_nt(qj, k_ref[c * DIFF_CHUNK:(c + 1) * DIFF_CHUNK, cols])
            for u in range(per_chunk):
                kb = c * per_chunk + u
                blk = s[:, u * LANES:(u + 1) * LANES] + tbl[t0 + kb]
                sbuf[j, kb] = blk
                mx = blk if mx is None else jnp.maximum(mx, blk)
        blk = _dot_nt(qj, kmp[:, cols]) + meta_mask
        sbuf[j, n_real] = blk
        mx = jnp.maximum(mx, blk)
        m = jnp.max(mx, -1, keepdims=True)
        acc = None
        for kb in range(n_real + 1):
            p = jnp.exp(sbuf[j, kb] - m)
            sbuf[j, kb] = p
            acc = p if acc is None else acc + p
        ls.append(jnp.sum(acc, -1, keepdims=True))
    r = lam * ls[0] / ls[1]
    o = None
    for c in range(n_chunk):
        a = jnp.concatenate(
            [sbuf[0, c * per_chunk + u] - r * sbuf[1, c * per_chunk + u] for u in range(per_chunk)],
            axis=1)
        part = _dot(a.astype(BF16), v_ref[c * DIFF_CHUNK:(c + 1) * DIFF_CHUNK, :])
        o = part if o is None else o + part
    a = sbuf[0, n_real] - r * sbuf[1, n_real]
    o = (o + _dot(a.astype(BF16), vmp[...])) / ls[0]
    y_ref[...] = finish(o, z_ref[...])


def _alibi_slopes(n_heads):
    return jnp.exp2(-8.0 * jnp.arange(1, n_heads + 1, dtype=F32) / n_heads)


def _diff_attention(qkvz, qkvz_m, lq1, lk1, lq2, lk2, subg, layer_idx, bn):
    width = 2 * B_HEADS * HEAD_DIM
    hw = 2 * HEAD_DIM
    n_blocks = SEQ // Q_TILE
    lambda_init = 0.8 - 0.6 * math.exp(-0.3 * layer_idx)

    def real(sec):
        return pl.BlockSpec((Q_TILE, hw), lambda h, b, n: (b * n_blocks + n, sec * B_HEADS + h))

    def full(sec):
        return pl.BlockSpec((SEQ, hw), lambda h, b, n: (b, sec * B_HEADS + h))

    def meta(sec):
        return pl.BlockSpec((N_META, hw), lambda h, b, n: (b, sec * B_HEADS + h))

    def vec(width_):
        return pl.BlockSpec((1, width_), lambda h, b, n: (0, 0))

    n_tbl = (SEQ - Q_TILE + SEQ) // LANES
    return pl.pallas_call(
        functools.partial(_diff_kernel, lambda_init=lambda_init),
        out_shape=(jax.ShapeDtypeStruct((bn * SEQ, width), BF16),
                   jax.ShapeDtypeStruct((bn * N_META, width), BF16)),
        grid=(B_HEADS, bn, n_blocks),
        in_specs=[
            pl.BlockSpec(memory_space=pltpu.SMEM),
            real(0), full(1), full(2), real(3), meta(0), meta(1), meta(2), meta(3),
            vec(HEAD_DIM), vec(HEAD_DIM), vec(HEAD_DIM), vec(HEAD_DIM), vec(hw),
        ],
        out_specs=(
            pl.BlockSpec((Q_TILE, hw), lambda h, b, n: (b * n_blocks + n, h)),
            pl.BlockSpec((N_META, hw), lambda h, b, n: (b, h)),
        ),
        scratch_shapes=[
            pltpu.VMEM((META_PAD, hw), BF16),
            pltpu.VMEM((META_PAD, hw), BF16),
            pltpu.VMEM((n_tbl, Q_TILE, LANES), F32),
            pltpu.VMEM((2, SEQ // LANES + 1, Q_TILE, LANES), F32),
        ],
        compiler_params=_compiler_params(3),
        name="diff_attention",
    )(_alibi_slopes(B_HEADS), qkvz, qkvz, qkvz, qkvz, qkvz_m, qkvz_m, qkvz_m, qkvz_m,
      lq1.reshape(1, -1), lk1.reshape(1, -1), lq2.reshape(1, -1), lk2.reshape(1, -1),
      subg.reshape(1, -1))


def _swa_tables():
    qi = np.arange(Q_TILE)[:, None]
    kj = np.arange(SWA_WIN)[None, :]
    dist = np.abs(qi + C_WINDOW - kj)
    near = dist <= C_WINDOW
    valid = np.stack([near & (kj >= C_WINDOW), near, near & (kj < C_WINDOW + Q_TILE)])
    return dist.astype(np.float32), valid.astype(np.float32)


def _swa_kernel(slopes_ref, sink_ref, q_ref, k_ref, v_ref, z_ref, qm_ref, km_ref, vm_ref, zm_ref,
                dist_ref, valid_ref, y_ref, ym_ref, kpad, vpad, kmp, vmp, comb, *, n_blocks):
    kh = pl.program_id(0)
    b = pl.program_id(1)
    n = pl.program_id(2)
    meta_mask = _meta_lane_mask()

    def stack_heads(x):
        return jnp.concatenate(
            [x[:, gq * HEAD_DIM:(gq + 1) * HEAD_DIM] for gq in range(C_GROUP)], axis=0)

    def sink_column(rows):
        return jnp.concatenate(
            [jnp.full((rows, 1), sink_ref[kh * C_GROUP + gq], F32) for gq in range(C_GROUP)], axis=0)

    def gate_and_unstack(o, z, rows):
        z = z.astype(F32)
        return jnp.concatenate(
            [o[gq * rows:(gq + 1) * rows] * _silu(z[:, gq * HEAD_DIM:(gq + 1) * HEAD_DIM])
             for gq in range(C_GROUP)], axis=1).astype(BF16)

    @pl.when(jnp.logical_and(b == 0, n == 0))
    def _():
        for c in range(3):
            for gq in range(C_GROUP):
                slope = slopes_ref[kh * C_GROUP + gq]
                comb[c, gq * Q_TILE:(gq + 1) * Q_TILE, :] = jnp.where(
                    valid_ref[c] > 0.5, -slope * dist_ref[...], NEG_INF)

    @pl.when(n == 0)
    def _():
        zeros = jnp.zeros((C_WINDOW, HEAD_DIM), BF16)
        for pad, src in ((kpad, k_ref), (vpad, v_ref)):
            pad[0:C_WINDOW] = zeros
            pad[C_WINDOW:C_WINDOW + SEQ] = src[...]
            pad[C_WINDOW + SEQ:SEQ + 2 * C_WINDOW] = zeros
        for pad, src in ((kmp, km_ref), (vmp, vm_ref)):
            pad[...] = jnp.zeros((META_PAD, HEAD_DIM), BF16)
            pad[0:N_META] = src[...]
        s = _dot_nt(stack_heads(qm_ref[...]), kmp[...]) + meta_mask
        sk = sink_column(N_META)
        m = jnp.maximum(jnp.max(s, -1, keepdims=True), sk)
        p = jnp.exp(s - m)
        l = jnp.sum(p, -1, keepdims=True) + jnp.exp(sk - m)
        o = _dot(p.astype(BF16), vmp[...]) / l
        ym_ref[...] = gate_and_unstack(o, zm_ref[...], N_META)

    start = pl.multiple_of(n * Q_TILE, Q_TILE)
    kw = kpad[pl.ds(start, SWA_WIN), :]
    vw = vpad[pl.ds(start, SWA_WIN), :]
    kind = jnp.where(n == 0, 0, jnp.where(n == n_blocks - 1, 2, 1))
    q = stack_heads(q_ref[...])
    s_w = _dot_nt(q, kw) + comb[kind]
    s_m = _dot_nt(q, kmp[...]) + meta_mask
    sk = sink_column(Q_TILE)
    m = jnp.maximum(jnp.maximum(jnp.max(s_w, -1, keepdims=True), jnp.max(s_m, -1, keepdims=True)), sk)
    p_w = jnp.exp(s_w - m)
    p_m = jnp.exp(s_m - m)
    l = jnp.sum(p_w, -1, keepdims=True) + jnp.sum(p_m, -1, keepdims=True) + jnp.exp(sk - m)
    o = (_dot(p_w.astype(BF16), vw) + _dot(p_m.astype(BF16), vmp[...])) / l
    y_ref[...] = gate_and_unstack(o, z_ref[...], Q_TILE)


def _swa_attention(qkvz, qkvz_m, sink, bn):
    width = C_HEADS * HEAD_DIM
    gw = C_GROUP * HEAD_DIM
    n_blocks = SEQ // Q_TILE
    assert n_blocks >= 2
    k_blk0 = width // HEAD_DIM
    v_blk0 = k_blk0 + C_KV_HEADS
    z_blk0 = (width + 2 * C_KV_HEADS * HEAD_DIM) // gw
    dist, valid = _swa_tables()

    def q_like(blk0):
        return pl.BlockSpec((Q_TILE, gw), lambda kh, b, n: (b * n_blocks + n, blk0 + kh))

    def kv(blk0):
        return pl.BlockSpec((SEQ, HEAD_DIM), lambda kh, b, n: (b, blk0 + kh))

    def meta(cols, blk0):
        return pl.BlockSpec((N_META, cols), lambda kh, b, n: (b, blk0 + kh))

    smem = pl.BlockSpec(memory_space=pltpu.SMEM)
    return pl.pallas_call(
        functools.partial(_swa_kernel, n_blocks=n_blocks),
        out_shape=(jax.ShapeDtypeStruct((bn * SEQ, width), BF16),
                   jax.ShapeDtypeStruct((bn * N_META, width), BF16)),
        grid=(C_KV_HEADS, bn, n_blocks),
        in_specs=[
            smem, smem,
            q_like(0), kv(k_blk0), kv(v_blk0), q_like(z_blk0),
            meta(gw, 0), meta(HEAD_DIM, k_blk0), meta(HEAD_DIM, v_blk0), meta(gw, z_blk0),
            pl.BlockSpec((Q_TILE, SWA_WIN), lambda kh, b, n: (0, 0)),
            pl.BlockSpec((3, Q_TILE, SWA_WIN), lambda kh, b, n: (0, 0, 0)),
        ],
        out_specs=(
            pl.BlockSpec((Q_TILE, gw), lambda kh, b, n: (b * n_blocks + n, kh)),
            pl.BlockSpec((N_META, gw), lambda kh, b, n: (b, kh)),
        ),
        scratch_shapes=[
            pltpu.VMEM((SEQ + 2 * C_WINDOW, HEAD_DIM), BF16),
            pltpu.VMEM((SEQ + 2 * C_WINDOW, HEAD_DIM), BF16),
            pltpu.VMEM((META_PAD, HEAD_DIM), BF16),
            pltpu.VMEM((META_PAD, HEAD_DIM), BF16),
            pltpu.VMEM((3, C_GROUP * Q_TILE, SWA_WIN), F32),
        ],
        compiler_params=_compiler_params(3),
        name="swa_attention",
    )(_alibi_slopes(C_HEADS), sink.astype(F32), qkvz, qkvz, qkvz, qkvz,
      qkvz_m, qkvz_m, qkvz_m, qkvz_m, jnp.asarray(dist), jnp.asarray(valid))


def kernel(x, meta_tokens, w_in_a, rpb_a, w_in_b, lam_q1_b, lam_k1_b, lam_q2_b, lam_k2_b,
           subln_g_b, w_in_c, sink_c, w_out, ln_g, ln_b):
    bn, seq, d = x.shape
    assert seq == SEQ and d == D_MODEL
    h = x.reshape(bn * seq, d)
    hm = jnp.tile(meta_tokens.astype(F32), (bn, 1))
    h_bf, hm_bf = h.astype(BF16), hm.astype(BF16)
    w_out_bf = w_out.astype(BF16)
    for i in range(DEPTH):
        kind, j = i % 3, i // 3
        if kind == 0:
            qkvz, qkvz_m = _inproj(h_bf, hm_bf, w_in_a, j, A_HEADS * HEAD_DIM)
            y, ym = _na_attention(qkvz, qkvz_m, rpb_a[j], bn)
        elif kind == 1:
            qkvz, qkvz_m = _inproj(h_bf, hm_bf, w_in_b, j, 2 * B_HEADS * HEAD_DIM)
            y, ym = _diff_attention(qkvz, qkvz_m, lam_q1_b[j], lam_k1_b[j], lam_q2_b[j],
                                    lam_k2_b[j], subln_g_b[j], i, bn)
        else:
            qkvz, qkvz_m = _inproj(h_bf, hm_bf, w_in_c, j, C_HEADS * HEAD_DIM)
            y, ym = _swa_attention(qkvz, qkvz_m, sink_c[j], bn)
        last = i == DEPTH - 1
        outs = _outproj_ln(y, ym, w_out_bf, i, h, hm, ln_g[i], ln_b[i], not last)
        if last:
            h, hm = outs
        else:
            h, hm, h_bf, hm_bf = outs
    return h.reshape(bn, seq, d)
```

```python
import functools
import math

import jax
import jax.numpy as jnp
import numpy as np
from jax import lax
from jax.experimental import pallas as pl
from jax.experimental.pallas import tpu as pltpu

F32 = jnp.float32
BF16 = jnp.bfloat16

D_MODEL = 2048
SEQ = 2048
DEPTH = 4
N_META = 16
GRID_W = 64
GRID_ROWS = SEQ // GRID_W
NA_ROWS = 8
NA_COLS = 16
HEAD_DIM = 128
A_HEADS = 16
B_HEADS = 8
C_HEADS = 16
C_KV_HEADS = 4
C_GROUP = C_HEADS // C_KV_HEADS
C_WINDOW = 128
ALPHA = (2 * DEPTH) ** 0.25
LN_EPS = 1e-5
RMS_EPS = 1e-5
NEG_INF = -1e30

LANES = 128
META_PAD = LANES
VMEM_LIMIT = 56 * 1024 * 1024

PROJ_TM = 1024
PROJ_TN = 512
LN_TM = 256
Q_TILE = 256
NA_HEADS_PER_STEP = 4
NA_GROUP_ROWS = Q_TILE // GRID_W
NA_WIN_ROWS = NA_GROUP_ROWS + NA_ROWS
NA_WIN = NA_WIN_ROWS * GRID_W
NA_PAD = (NA_ROWS // 2) * GRID_W
SWA_WIN = Q_TILE + 2 * C_WINDOW
DIFF_KC = 256


def _dot_nt(a, b):
    return lax.dot_general(a, b, (((1,), (1,)), ((), ())), preferred_element_type=F32)


def _dot(a, b):
    return jnp.dot(a, b, preferred_element_type=F32)


def _silu(z):
    return z / (1.0 + jnp.exp(-z))


def _meta_lane_mask():
    lane = lax.broadcasted_iota(jnp.int32, (1, META_PAD), 1)
    return jnp.where(lane < N_META, 0.0, NEG_INF).astype(F32)


def _compiler_params(n_axes):
    return pltpu.CompilerParams(
        dimension_semantics=("arbitrary",) * n_axes, vmem_limit_bytes=VMEM_LIMIT)


def _alibi_slopes(n_heads):
    return jnp.asarray(np.exp2(-8.0 * np.arange(1, n_heads + 1) / n_heads), F32)


def _inproj_kernel(x_ref, xm_ref, w_ref, o_ref, om_ref, wbf_ref, *, n_q_tiles, q_scale):
    j = pl.program_id(0)
    i = pl.program_id(1)
    scale = jnp.where(j < n_q_tiles, q_scale, 1.0).astype(F32)

    def put(dst, acc):
        for c in range(PROJ_TN // LANES):
            dst[c] = (acc[:, c * LANES:(c + 1) * LANES] * scale).astype(BF16)

    @pl.when(i == 0)
    def _():
        wbf_ref[...] = w_ref[...].astype(BF16)
        put(om_ref, _dot(xm_ref[...], wbf_ref[...]))

    put(o_ref, _dot(x_ref[...], wbf_ref[...]))


def _inproj(x_bf, xm_bf, w_stack, layer, q_cols):
    m, d = x_bf.shape
    mm = xm_bf.shape[0]
    n = w_stack.shape[2]
    tm = min(PROJ_TM, m)
    cb = PROJ_TN // LANES
    assert m % tm == 0 and n % PROJ_TN == 0 and q_cols % PROJ_TN == 0
    kern = functools.partial(_inproj_kernel, n_q_tiles=q_cols // PROJ_TN,
                             q_scale=HEAD_DIM ** -0.5)
    return pl.pallas_call(
        kern,
        out_shape=(jax.ShapeDtypeStruct((n // LANES, m, LANES), BF16),
                   jax.ShapeDtypeStruct((n // LANES, mm, LANES), BF16)),
        grid=(n // PROJ_TN, m // tm),
        in_specs=[
            pl.BlockSpec((tm, d), lambda j, i: (i, 0)),
            pl.BlockSpec((mm, d), lambda j, i: (0, 0)),
            pl.BlockSpec((None, d, PROJ_TN), lambda j, i: (layer, 0, j)),
        ],
        out_specs=(
            pl.BlockSpec((cb, tm, LANES), lambda j, i: (j, i, 0)),
            pl.BlockSpec((cb, mm, LANES), lambda j, i: (j, 0, 0)),
        ),
        scratch_shapes=[pltpu.VMEM((d, PROJ_TN), BF16)],
        compiler_params=_compiler_params(2),
        name="inproj",
    )(x_bf, xm_bf, w_stack)


def _outproj_ln_kernel(y_ref, ym_ref, w_ref, h_ref, hm_ref, g_ref, b_ref, *out_refs, emit_bf16):
    def layer_norm(y_cb, h):
        y = jnp.concatenate([y_cb[c] for c in range(y_cb.shape[0])], axis=1)
        t = ALPHA * h + _dot(y, w_ref[...])
        mu = jnp.mean(t, -1, keepdims=True)
        d = t - mu
        var = jnp.mean(d * d, -1, keepdims=True)
        return d * lax.rsqrt(var + LN_EPS) * g_ref[...] + b_ref[...]

    if emit_bf16:
        o_ref, om_ref, obf_ref, ombf_ref = out_refs
    else:
        o_ref, om_ref = out_refs

    @pl.when(pl.program_id(0) == 0)
    def _():
        r = layer_norm(ym_ref, hm_ref[...])
        om_ref[...] = r
        if emit_bf16:
            ombf_ref[...] = r.astype(BF16)

    r = layer_norm(y_ref, h_ref[...])
    o_ref[...] = r
    if emit_bf16:
        obf_ref[...] = r.astype(BF16)


def _outproj_ln(y, ym, w_stack_bf, layer, h, hm, g, b, emit_bf16):
    m, d = h.shape
    mm = hm.shape[0]
    cb = d // LANES
    assert m % LN_TM == 0
    row = pl.BlockSpec((LN_TM, d), lambda i: (i, 0))
    meta = pl.BlockSpec((mm, d), lambda i: (0, 0))
    vec = pl.BlockSpec((1, d), lambda i: (0, 0))
    out_shape = [jax.ShapeDtypeStruct((m, d), F32), jax.ShapeDtypeStruct((mm, d), F32)]
    out_specs = [row, meta]
    if emit_bf16:
        out_shape += [jax.ShapeDtypeStruct((m, d), BF16), jax.ShapeDtypeStruct((mm, d), BF16)]
        out_specs += [row, meta]
    return pl.pallas_call(
        functools.partial(_outproj_ln_kernel, emit_bf16=emit_bf16),
        out_shape=tuple(out_shape),
        grid=(m // LN_TM,),
        in_specs=[
            pl.BlockSpec((cb, LN_TM, LANES), lambda i: (0, i, 0)),
            pl.BlockSpec((cb, mm, LANES), lambda i: (0, 0, 0)),
            pl.BlockSpec((None, d, d), lambda i: (layer, 0, 0)),
            row, meta, vec, vec,
        ],
        out_specs=tuple(out_specs),
        compiler_params=_compiler_params(1),
        name="outproj_ln",
    )(y, ym, w_stack_bf, h, hm, g.reshape(1, d), b.reshape(1, d))


def _na_tables():
    qa = np.arange(NA_GROUP_ROWS)
    ke = np.arange(NA_WIN_ROWS)
    dr_idx = ke[None, :] - NA_ROWS // 2 - qa[:, None] + NA_ROWS - 1
    n_groups = GRID_ROWS // NA_GROUP_ROWS
    oks = []
    for g in range(n_groups):
        r = NA_GROUP_ROWS * g + qa
        r0 = np.clip(r - NA_ROWS // 2, 0, GRID_ROWS - NA_ROWS)
        kr = NA_GROUP_ROWS * g - NA_ROWS // 2 + ke
        oks.append((kr[None, :] >= r0[:, None]) & (kr[None, :] < r0[:, None] + NA_ROWS))
    for g in range(1, n_groups - 1):
        assert np.array_equal(oks[g], oks[1])
    row_ok = np.stack([oks[0], oks[1], oks[n_groups - 1]])
    assert dr_idx[row_ok.any(0)].min() >= 0 and dr_idx[row_ok.any(0)].max() <= 2 * NA_ROWS - 2
    return dr_idx, row_ok


_NA_TABLES = _na_tables()


def _na_build_bias(rpb_ref, toep, comb):
    dr_idx, row_ok = _NA_TABLES
    qc = lax.broadcasted_iota(jnp.int32, (GRID_W, LANES), 0)
    lane = lax.broadcasted_iota(jnp.int32, (GRID_W, LANES), 1)
    kc = lane & (GRID_W - 1)
    c0 = jnp.clip(qc - NA_COLS // 2, 0, GRID_W - NA_COLS)
    col_ok = jnp.logical_and(kc >= c0, kc < c0 + NA_COLS)
    dc_idx = jnp.clip(kc - qc + NA_COLS - 1, 0, LANES - 1)
    for dr in range(2 * NA_ROWS - 1):
        row = jnp.broadcast_to(rpb_ref[dr:dr + 1, :], (GRID_W, LANES))
        toep[dr] = jnp.where(col_ok, jnp.take_along_axis(row, dc_idx, axis=1), NEG_INF)
    masked = jnp.full((GRID_W, LANES), NEG_INF, F32)
    left_half = lane < GRID_W
    for kind in range(3):
        for a in range(NA_GROUP_ROWS):
            for ep in range(NA_WIN_ROWS // 2):
                halves = [toep[int(dr_idx[a, e])] if row_ok[kind, a, e] else masked
                          for e in (2 * ep, 2 * ep + 1)]
                comb[kind, a * GRID_W:(a + 1) * GRID_W, ep * LANES:(ep + 1) * LANES] = jnp.where(
                    left_half, halves[0], halves[1])


def _na_kernel(q_ref, k_ref, v_ref, z_ref, qm_ref, km_ref, vm_ref, zm_ref, rpb_ref,
               y_ref, ym_ref, kpad, vpad, kmp, vmp, toep, comb, *, n_groups):
    b = pl.program_id(1)
    g = pl.program_id(2)
    heads = q_ref.shape[0]
    meta_mask = _meta_lane_mask()

    @pl.when(jnp.logical_and(b == 0, g == 0))
    def _():
        for hh in range(heads):
            _na_build_bias(rpb_ref.at[hh], toep, comb.at[hh])

    @pl.when(g == 0)
    def _():
        zeros = jnp.zeros((NA_PAD, HEAD_DIM), BF16)
        for hh in range(heads):
            for pad, src in ((kpad, k_ref), (vpad, v_ref)):
                pad[hh, 0:NA_PAD] = zeros
                pad[hh, NA_PAD:NA_PAD + SEQ] = src[hh]
                pad[hh, NA_PAD + SEQ:NA_PAD + SEQ + NA_PAD] = zeros
            for pad, src in ((kmp, km_ref), (vmp, vm_ref)):
                pad[hh] = jnp.zeros((META_PAD, HEAD_DIM), BF16)
                pad[hh, 0:N_META] = src[hh]
            s = _dot_nt(qm_ref[hh], kmp[hh]) + meta_mask
            p = jnp.exp(s - jnp.max(s, -1, keepdims=True))
            o = _dot(p.astype(BF16), vmp[hh]) / jnp.sum(p, -1, keepdims=True)
            ym_ref[hh] = (o * _silu(zm_ref[hh].astype(F32))).astype(BF16)

    start = pl.multiple_of(g * Q_TILE, Q_TILE)
    kind = jnp.where(g == 0, 0, jnp.where(g == n_groups - 1, 2, 1))
    for hh in range(heads):
        kw = kpad[hh, pl.ds(start, NA_WIN), :]
        vw = vpad[hh, pl.ds(start, NA_WIN), :]
        q = q_ref[hh]
        s_w = _dot_nt(q, kw) + comb[hh, kind]
        s_m = _dot_nt(q, kmp[hh]) + meta_mask
        m = jnp.maximum(jnp.max(s_w, -1, keepdims=True), jnp.max(s_m, -1, keepdims=True))
        p_w = jnp.exp(s_w - m)
        p_m = jnp.exp(s_m - m)
        l = jnp.sum(p_w, -1, keepdims=True) + jnp.sum(p_m, -1, keepdims=True)
        o = (_dot(p_w.astype(BF16), vw) + _dot(p_m.astype(BF16), vmp[hh])) / l
        y_ref[hh] = (o * _silu(z_ref[hh].astype(F32))).astype(BF16)


def _na_attention(qkvz, qkvz_m, rpb, bn):
    hp = NA_HEADS_PER_STEP
    n_groups = SEQ // Q_TILE
    n_dr, n_dc = rpb.shape[1:]
    rpb_pad = jnp.pad(rpb.astype(F32), ((0, 0), (0, 16 - n_dr), (0, LANES - n_dc)))
    sec = A_HEADS // hp

    def real(s):
        return pl.BlockSpec((hp, Q_TILE, HEAD_DIM), lambda h, b, g: (s * sec + h, b * n_groups + g, 0))

    def full(s):
        return pl.BlockSpec((hp, SEQ, HEAD_DIM), lambda h, b, g: (s * sec + h, b, 0))

    def meta(s):
        return pl.BlockSpec((hp, N_META, HEAD_DIM), lambda h, b, g: (s * sec + h, b, 0))

    return pl.pallas_call(
        functools.partial(_na_kernel, n_groups=n_groups),
        out_shape=(jax.ShapeDtypeStruct((A_HEADS, bn * SEQ, HEAD_DIM), BF16),
                   jax.ShapeDtypeStruct((A_HEADS, bn * N_META, HEAD_DIM), BF16)),
        grid=(sec, bn, n_groups),
        in_specs=[
            real(0), full(1), full(2), real(3), meta(0), meta(1), meta(2), meta(3),
            pl.BlockSpec((hp, 16, LANES), lambda h, b, g: (h, 0, 0)),
        ],
        out_specs=(real(0), meta(0)),
        scratch_shapes=[
            pltpu.VMEM((hp, SEQ + 2 * NA_PAD, HEAD_DIM), BF16),
            pltpu.VMEM((hp, SEQ + 2 * NA_PAD, HEAD_DIM), BF16),
            pltpu.VMEM((hp, META_PAD, HEAD_DIM), BF16),
            pltpu.VMEM((hp, META_PAD, HEAD_DIM), BF16),
            pltpu.VMEM((2 * NA_ROWS - 1, GRID_W, LANES), F32),
            pltpu.VMEM((hp, 3, Q_TILE, NA_WIN), F32),
        ],
        compiler_params=_compiler_params(3),
        name="na_attention",
    )(qkvz, qkvz, qkvz, qkvz, qkvz_m, qkvz_m, qkvz_m, qkvz_m, rpb_pad)


def _diff_kernel(slopes_ref, q_ref, k_ref, v_ref, z_ref, qm_ref, km_ref, vm_ref, zm_ref,
                 lq1_ref, lk1_ref, lq2_ref, lk2_ref, subg_ref, y_ref, ym_ref,
                 kaug, kmp, vmp, diag, sbuf, *, lambda_init):
    h = pl.program_id(0)
    b = pl.program_id(1)
    n = pl.program_id(2)
    rows = Q_TILE + N_META
    n_chunk = SEQ // DIFF_KC
    slope = slopes_ref[h]
    meta_mask = _meta_lane_mask()
    lam = (jnp.exp(jnp.sum(lq1_ref[...] * lk1_ref[...], -1, keepdims=True))
           - jnp.exp(jnp.sum(lq2_ref[...] * lk2_ref[...], -1, keepdims=True)) + lambda_init)

    @pl.when(jnp.logical_and(b == 0, n == 0))
    def _():
        kpos = lax.broadcasted_iota(jnp.int32, (SEQ, LANES), 0)
        lane = lax.broadcasted_iota(jnp.int32, (SEQ, LANES), 1)
        k_lo = (kpos & (DIFF_KC - 1)).astype(F32)
        k_hi = (kpos - (kpos & (DIFF_KC - 1))).astype(F32)
        feat = jnp.where(lane < 2, -slope,
                         jnp.where(lane == 2, slope * k_hi, jnp.where(lane == 3, slope * k_lo, 0.0)))
        for j in range(2):
            kaug[j, :, HEAD_DIM:2 * HEAD_DIM] = feat.astype(BF16)
        qi = lax.broadcasted_iota(jnp.int32, (rows, DIFF_KC), 0)
        kj = lax.broadcasted_iota(jnp.int32, (rows, DIFF_KC), 1)
        diag[...] = jnp.where(qi < Q_TILE, -slope * jnp.abs(qi - kj).astype(F32), 0.0)

    @pl.when(n == 0)
    def _():
        vmp[...] = jnp.zeros((META_PAD, 2 * HEAD_DIM), BF16)
        for j in range(2):
            kaug[j, :, 0:HEAD_DIM] = k_ref[j]
            kmp[j] = jnp.zeros((META_PAD, HEAD_DIM), BF16)
            kmp[j, 0:N_META] = km_ref[j]
            vmp[0:N_META, j * HEAD_DIM:(j + 1) * HEAD_DIM] = vm_ref[j]

    row = lax.broadcasted_iota(jnp.int32, (rows, LANES), 0)
    lane = lax.broadcasted_iota(jnp.int32, (rows, LANES), 1)
    q_hi = (n * Q_TILE).astype(F32)
    qfeat = jnp.where(lane == 0, q_hi, jnp.where(lane == 1, row.astype(F32),
                                                  jnp.where(lane < 4, 1.0, 0.0)))
    qfeat = jnp.where(row < Q_TILE, qfeat, 0.0)
    feat_before = qfeat.astype(BF16)
    feat_after = (-qfeat).astype(BF16)
    feat_diag = jnp.zeros((rows, LANES), BF16)
    chunks = [(n + d) & (n_chunk - 1) for d in range(n_chunk)]

    maps = range(2)
    qs = [jnp.concatenate([q_ref[j], qm_ref[j]], axis=0) for j in maps]
    mx = [None, None]
    for d, c in enumerate(chunks):
        start = pl.multiple_of(c * DIFF_KC, DIFF_KC)
        f = feat_diag if d == 0 else jnp.where(c < n, feat_before, feat_after)
        for j in maps:
            s = _dot_nt(jnp.concatenate([qs[j], f], axis=1), kaug[j, pl.ds(start, DIFF_KC), :])
            if d == 0:
                s = s + diag[...]
            sbuf[j, d] = s
            mx[j] = s if mx[j] is None else jnp.maximum(mx[j], s)
    s_m = [_dot_nt(qs[j], kmp[j]) + meta_mask for j in maps]
    m = [jnp.maximum(jnp.max(mx[j], -1, keepdims=True), jnp.max(s_m[j], -1, keepdims=True))
         for j in maps]
    acc = [None, None]
    o = [None, None]
    for d, c in enumerate(chunks):
        start = pl.multiple_of(c * DIFF_KC, DIFF_KC)
        v_chunk = jnp.concatenate(
            [v_ref[0, pl.ds(start, DIFF_KC), :], v_ref[1, pl.ds(start, DIFF_KC), :]], axis=1)
        for j in maps:
            p = jnp.exp(sbuf[j, d] - m[j])
            acc[j] = p if acc[j] is None else acc[j] + p
            part = _dot(p.astype(BF16), v_chunk)
            o[j] = part if o[j] is None else o[j] + part
    outs = []
    for j in maps:
        p_m = jnp.exp(s_m[j] - m[j])
        l = jnp.sum(acc[j], -1, keepdims=True) + jnp.sum(p_m, -1, keepdims=True)
        outs.append((o[j] + _dot(p_m.astype(BF16), vmp[...])) / l)

    o = outs[0] - lam * outs[1]
    o = o * lax.rsqrt(jnp.mean(o * o, -1, keepdims=True) + RMS_EPS)
    o = o * subg_ref[...] * (1.0 - lambda_init)
    z = jnp.concatenate([jnp.concatenate([z_ref[0], z_ref[1]], axis=1),
                         jnp.concatenate([zm_ref[0], zm_ref[1]], axis=1)], axis=0)
    res = (o * _silu(z.astype(F32))).astype(BF16)
    for j in range(2):
        y_ref[j] = res[:Q_TILE, j * HEAD_DIM:(j + 1) * HEAD_DIM]
        ym_ref[j] = res[Q_TILE:, j * HEAD_DIM:(j + 1) * HEAD_DIM]


def _diff_attention(qkvz, qkvz_m, lq1, lk1, lq2, lk2, subg, layer_idx, bn):
    n_blocks = SEQ // Q_TILE
    assert SEQ // DIFF_KC == n_blocks and Q_TILE == DIFF_KC
    lambda_init = 0.8 - 0.6 * math.exp(-0.3 * layer_idx)

    def real(s):
        return pl.BlockSpec((2, Q_TILE, HEAD_DIM), lambda h, b, n: (s * B_HEADS + h, b * n_blocks + n, 0))

    def full(s):
        return pl.BlockSpec((2, SEQ, HEAD_DIM), lambda h, b, n: (s * B_HEADS + h, b, 0))

    def meta(s):
        return pl.BlockSpec((2, N_META, HEAD_DIM), lambda h, b, n: (s * B_HEADS + h, b, 0))

    def vec(width):
        return pl.BlockSpec((1, width), lambda h, b, n: (0, 0))

    rows = Q_TILE + N_META
    return pl.pallas_call(
        functools.partial(_diff_kernel, lambda_init=lambda_init),
        out_shape=(jax.ShapeDtypeStruct((2 * B_HEADS, bn * SEQ, HEAD_DIM), BF16),
                   jax.ShapeDtypeStruct((2 * B_HEADS, bn * N_META, HEAD_DIM), BF16)),
        grid=(B_HEADS, bn, n_blocks),
        in_specs=[
            pl.BlockSpec(memory_space=pltpu.SMEM),
            real(0), full(1), full(2), real(3), meta(0), meta(1), meta(2), meta(3),
            vec(HEAD_DIM), vec(HEAD_DIM), vec(HEAD_DIM), vec(HEAD_DIM), vec(2 * HEAD_DIM),
        ],
        out_specs=(real(0), meta(0)),
        scratch_shapes=[
            pltpu.VMEM((2, SEQ, 2 * HEAD_DIM), BF16),
            pltpu.VMEM((2, META_PAD, HEAD_DIM), BF16),
            pltpu.VMEM((META_PAD, 2 * HEAD_DIM), BF16),
            pltpu.VMEM((rows, DIFF_KC), F32),
            pltpu.VMEM((2, SEQ // DIFF_KC, rows, DIFF_KC), F32),
        ],
        compiler_params=_compiler_params(3),
        name="diff_attention",
    )(_alibi_slopes(B_HEADS), qkvz, qkvz, qkvz, qkvz, qkvz_m, qkvz_m, qkvz_m, qkvz_m,
      lq1.reshape(1, -1), lk1.reshape(1, -1), lq2.reshape(1, -1), lk2.reshape(1, -1),
      subg.reshape(1, -1))


def _swa_tables():
    qi = np.arange(Q_TILE)[:, None]
    kj = np.arange(SWA_WIN)[None, :]
    dist = np.abs(qi + C_WINDOW - kj)
    near = dist <= C_WINDOW
    valid = np.stack([near & (kj >= C_WINDOW), near, near & (kj < C_WINDOW + Q_TILE)])
    return dist.astype(np.float32), valid.astype(np.float32)


def _swa_kernel(slopes_ref, sink_ref, q_ref, k_ref, v_ref, z_ref, qm_ref, km_ref, vm_ref, zm_ref,
                dist_ref, valid_ref, y_ref, ym_ref, kpad, vpad, kmp, vmp, comb, *, n_blocks):
    kh = pl.program_id(0)
    b = pl.program_id(1)
    n = pl.program_id(2)
    meta_mask = _meta_lane_mask()

    def stack_heads(x_ref):
        return jnp.concatenate([x_ref[gq] for gq in range(C_GROUP)], axis=0)

    def sink_column(rows):
        return jnp.concatenate(
            [jnp.full((rows, 1), sink_ref[kh * C_GROUP + gq], F32) for gq in range(C_GROUP)], axis=0)

    def gate_and_store(dst_ref, o, z_ref_, rows):
        for gq in range(C_GROUP):
            dst_ref[gq] = (o[gq * rows:(gq + 1) * rows] * _silu(z_ref_[gq].astype(F32))).astype(BF16)

    @pl.when(jnp.logical_and(b == 0, n == 0))
    def _():
        for c in range(3):
            for gq in range(C_GROUP):
                slope = slopes_ref[kh * C_GROUP + gq]
                comb[c, gq * Q_TILE:(gq + 1) * Q_TILE, :] = jnp.where(
                    valid_ref[c] > 0.5, -slope * dist_ref[...], NEG_INF)

    @pl.when(n == 0)
    def _():
        zeros = jnp.zeros((C_WINDOW, HEAD_DIM), BF16)
        for pad, src in ((kpad, k_ref), (vpad, v_ref)):
            pad[0:C_WINDOW] = zeros
            pad[C_WINDOW:C_WINDOW + SEQ] = src[...]
            pad[C_WINDOW + SEQ:SEQ + 2 * C_WINDOW] = zeros
        for pad, src in ((kmp, km_ref), (vmp, vm_ref)):
            pad[...] = jnp.zeros((META_PAD, HEAD_DIM), BF16)
            pad[0:N_META] = src[...]
        s = _dot_nt(stack_heads(qm_ref), kmp[...]) + meta_mask
        sk = sink_column(N_META)
        m = jnp.maximum(jnp.max(s, -1, keepdims=True), sk)
        p = jnp.exp(s - m)
        l = jnp.sum(p, -1, keepdims=True) + jnp.exp(sk - m)
        o = _dot(p.astype(BF16), vmp[...]) / l
        gate_and_store(ym_ref, o, zm_ref, N_META)

    start = pl.multiple_of(n * Q_TILE, Q_TILE)
    kw = kpad[pl.ds(start, SWA_WIN), :]
    vw = vpad[pl.ds(start, SWA_WIN), :]
    kind = jnp.where(n == 0, 0, jnp.where(n == n_blocks - 1, 2, 1))
    q = stack_heads(q_ref)
    s_w = _dot_nt(q, kw) + comb[kind]
    s_m = _dot_nt(q, kmp[...]) + meta_mask
    sk = sink_column(Q_TILE)
    m = jnp.maximum(jnp.maximum(jnp.max(s_w, -1, keepdims=True), jnp.max(s_m, -1, keepdims=True)), sk)
    p_w = jnp.exp(s_w - m)
    p_m = jnp.exp(s_m - m)
    l = jnp.sum(p_w, -1, keepdims=True) + jnp.sum(p_m, -1, keepdims=True) + jnp.exp(sk - m)
    o = (_dot(p_w.astype(BF16), vw) + _dot(p_m.astype(BF16), vmp[...])) / l
    gate_and_store(y_ref, o, z_ref, Q_TILE)


def _swa_attention(qkvz, qkvz_m, sink, bn):
    n_blocks = SEQ // Q_TILE
    assert n_blocks >= 2
    k_cb0 = C_HEADS
    v_cb0 = k_cb0 + C_KV_HEADS
    z_grp0 = (v_cb0 + C_KV_HEADS) // C_GROUP
    dist, valid = _swa_tables()

    def q_like(grp0):
        return pl.BlockSpec((C_GROUP, Q_TILE, HEAD_DIM), lambda kh, b, n: (grp0 + kh, b * n_blocks + n, 0))

    def q_like_meta(grp0):
        return pl.BlockSpec((C_GROUP, N_META, HEAD_DIM), lambda kh, b, n: (grp0 + kh, b, 0))

    def kv(cb0, rows):
        return pl.BlockSpec((None, rows, HEAD_DIM), lambda kh, b, n: (cb0 + kh, b, 0))

    smem = pl.BlockSpec(memory_space=pltpu.SMEM)
    return pl.pallas_call(
        functools.partial(_swa_kernel, n_blocks=n_blocks),
        out_shape=(jax.ShapeDtypeStruct((C_HEADS, bn * SEQ, HEAD_DIM), BF16),
                   jax.ShapeDtypeStruct((C_HEADS, bn * N_META, HEAD_DIM), BF16)),
        grid=(C_KV_HEADS, bn, n_blocks),
        in_specs=[
            smem, smem,
            q_like(0), kv(k_cb0, SEQ), kv(v_cb0, SEQ), q_like(z_grp0),
            q_like_meta(0), kv(k_cb0, N_META), kv(v_cb0, N_META), q_like_meta(z_grp0),
            pl.BlockSpec((Q_TILE, SWA_WIN), lambda kh, b, n: (0, 0)),
            pl.BlockSpec((3, Q_TILE, SWA_WIN), lambda kh, b, n: (0, 0, 0)),
        ],
        out_specs=(q_like(0), q_like_meta(0)),
        scratch_shapes=[
            pltpu.VMEM((SEQ + 2 * C_WINDOW, HEAD_DIM), BF16),
            pltpu.VMEM((SEQ + 2 * C_WINDOW, HEAD_DIM), BF16),
            pltpu.VMEM((META_PAD, HEAD_DIM), BF16),
            pltpu.VMEM((META_PAD, HEAD_DIM), BF16),
            pltpu.VMEM((3, C_GROUP * Q_TILE, SWA_WIN), F32),
        ],
        compiler_params=_compiler_params(3),
        name="swa_attention",
    )(_alibi_slopes(C_HEADS), sink.astype(F32), qkvz, qkvz, qkvz, qkvz,
      qkvz_m, qkvz_m, qkvz_m, qkvz_m, jnp.asarray(dist), jnp.asarray(valid))


def kernel(x, meta_tokens, w_in_a, rpb_a, w_in_b, lam_q1_b, lam_k1_b, lam_q2_b, lam_k2_b,
           subln_g_b, w_in_c, sink_c, w_out, ln_g, ln_b):
    bn, seq, d = x.shape
    assert seq == SEQ and d == D_MODEL
    h = x.reshape(bn * seq, d)
    hm = jnp.tile(meta_tokens.astype(F32), (bn, 1))
    h_bf, hm_bf = h.astype(BF16), hm.astype(BF16)
    w_out_bf = w_out.astype(BF16)
    for i in range(DEPTH):
        kind, j = i % 3, i // 3
        if kind == 0:
            qkvz, qkvz_m = _inproj(h_bf, hm_bf, w_in_a, j, A_HEADS * HEAD_DIM)
            y, ym = _na_attention(qkvz, qkvz_m, rpb_a[j], bn)
        elif kind == 1:
            qkvz, qkvz_m = _inproj(h_bf, hm_bf, w_in_b, j, 2 * B_HEADS * HEAD_DIM)
            y, ym = _diff_attention(qkvz, qkvz_m, lam_q1_b[j], lam_k1_b[j], lam_q2_b[j],
                                    lam_k2_b[j], subln_g_b[j], i, bn)
        else:
            qkvz, qkvz_m = _inproj(h_bf, hm_bf, w_in_c, j, C_HEADS * HEAD_DIM)
            y, ym = _swa_attention(qkvz, qkvz_m, sink_c[j], bn)
        last = i == DEPTH - 1
        outs = _outproj_ln(y, ym, w_out_bf, i, h, hm, ln_g[i], ln_b[i], not last)
        if last:
            h, hm = outs
        else:
            h, hm, h_bf, hm_bf = outs
    return h.reshape(bn, seq, d)
```

```python
import functools
import math

import jax
import jax.numpy as jnp
import numpy as np
from jax import lax
from jax.experimental import pallas as pl
from jax.experimental.pallas import tpu as pltpu

F32 = jnp.float32
BF16 = jnp.bfloat16

D_MODEL = 2048
SEQ = 2048
DEPTH = 4
N_META = 16
GRID_W = 64
GRID_ROWS = SEQ // GRID_W
NA_ROWS = 8
NA_COLS = 16
HEAD_DIM = 128
A_HEADS = 16
B_HEADS = 8
C_HEADS = 16
C_KV_HEADS = 4
C_GROUP = C_HEADS // C_KV_HEADS
C_WINDOW = 128
ALPHA = (2 * DEPTH) ** 0.25
LN_EPS = 1e-5
RMS_EPS = 1e-5
NEG_INF = -1e30
LOG2E = math.log2(math.e)
Q_SCALE_LOG2 = LOG2E * HEAD_DIM ** -0.5

LANES = 128
META_PAD = LANES
VMEM_LIMIT = 56 * 1024 * 1024

PROJ_TM = 1024
PROJ_TN = 1024
LN_TM = 512
LN_SUB = 128
Q_TILE = 256
NA_HEADS_PER_STEP = 4
NA_GROUP_ROWS = Q_TILE // GRID_W
NA_WIN_ROWS = NA_GROUP_ROWS + NA_ROWS
NA_WIN = NA_WIN_ROWS * GRID_W
NA_KEYS = NA_WIN + META_PAD
NA_PAD = (NA_ROWS // 2) * GRID_W
SWA_WIN = Q_TILE + 2 * C_WINDOW
SWA_KEYS = SWA_WIN + META_PAD
DIFF_KC = 256


def _dot_nt(a, b):
    return lax.dot_general(a, b, (((1,), (1,)), ((), ())), preferred_element_type=F32)


def _dot(a, b):
    return jnp.dot(a, b, preferred_element_type=F32)


def _silu(z):
    return z / (1.0 + jnp.exp(-z))


def _meta_lane_mask():
    lane = lax.broadcasted_iota(jnp.int32, (1, META_PAD), 1)
    return jnp.where(lane < N_META, 0.0, NEG_INF).astype(F32)


def _compiler_params(n_axes):
    return pltpu.CompilerParams(
        dimension_semantics=("arbitrary",) * n_axes, vmem_limit_bytes=VMEM_LIMIT)


def _alibi_slopes(n_heads):
    return jnp.asarray(np.exp2(-8.0 * np.arange(1, n_heads + 1) / n_heads), F32)


def _inproj_kernel(x_ref, xm_ref, w_ref, o_ref, om_ref, wbf_ref, *, n_q_tiles):
    j = pl.program_id(0)
    i = pl.program_id(1)
    scale = jnp.where(j < n_q_tiles, Q_SCALE_LOG2, 1.0).astype(F32)

    def put(dst, acc):
        for c in range(PROJ_TN // LANES):
            dst[c] = (acc[:, c * LANES:(c + 1) * LANES] * scale).astype(BF16)

    @pl.when(i == 0)
    def _():
        wbf_ref[...] = w_ref[...].astype(BF16)
        put(om_ref, _dot(xm_ref[...].astype(BF16), wbf_ref[...]))

    put(o_ref, _dot(x_ref[...].astype(BF16), wbf_ref[...]))


def _inproj(x, xm, w_stack, layer, q_cols):
    m, d = x.shape
    mm = xm.shape[0]
    n = w_stack.shape[2]
    tm = min(PROJ_TM, m)
    cb = PROJ_TN // LANES
    assert m % tm == 0 and n % PROJ_TN == 0 and q_cols % PROJ_TN == 0
    kern = functools.partial(_inproj_kernel, n_q_tiles=q_cols // PROJ_TN)
    return pl.pallas_call(
        kern,
        out_shape=(jax.ShapeDtypeStruct((n // LANES, m, LANES), BF16),
                   jax.ShapeDtypeStruct((n // LANES, mm, LANES), BF16)),
        grid=(n // PROJ_TN, m // tm),
        in_specs=[
            pl.BlockSpec((tm, d), lambda j, i: (i, 0)),
            pl.BlockSpec((mm, d), lambda j, i: (0, 0)),
            pl.BlockSpec((None, d, PROJ_TN), lambda j, i: (layer, 0, j)),
        ],
        out_specs=(
            pl.BlockSpec((cb, tm, LANES), lambda j, i: (j, i, 0)),
            pl.BlockSpec((cb, mm, LANES), lambda j, i: (j, 0, 0)),
        ),
        scratch_shapes=[pltpu.VMEM((d, PROJ_TN), BF16)],
        compiler_params=_compiler_params(2),
        name="inproj",
    )(x, xm, w_stack)


def _outproj_ln_kernel(y_ref, ym_ref, w_ref, h_ref, hm_ref, g_ref, b_ref, *out_refs, emit_bf16):
    def layer_norm(y_cb, rows, h):
        y = jnp.concatenate([y_cb[c, rows, :] for c in range(y_cb.shape[0])], axis=1)
        t = ALPHA * h + _dot(y, w_ref[...])
        mu = jnp.mean(t, -1, keepdims=True)
        d = t - mu
        var = jnp.mean(d * d, -1, keepdims=True)
        return d * lax.rsqrt(var + LN_EPS) * g_ref[...] + b_ref[...]

    if emit_bf16:
        o_ref, om_ref, obf_ref, ombf_ref = out_refs
    else:
        o_ref, om_ref = out_refs

    @pl.when(pl.program_id(0) == 0)
    def _():
        r = layer_norm(ym_ref, slice(None), hm_ref[...])
        om_ref[...] = r
        if emit_bf16:
            ombf_ref[...] = r.astype(BF16)

    for t in range(LN_TM // LN_SUB):
        rows = slice(t * LN_SUB, (t + 1) * LN_SUB)
        r = layer_norm(y_ref, rows, h_ref[rows, :])
        o_ref[rows, :] = r
        if emit_bf16:
            obf_ref[rows, :] = r.astype(BF16)


def _outproj_ln(y, ym, w_stack_bf, layer, h, hm, g, b, emit_bf16):
    m, d = h.shape
    mm = hm.shape[0]
    cb = d // LANES
    assert m % LN_TM == 0
    row = pl.BlockSpec((LN_TM, d), lambda i: (i, 0))
    meta = pl.BlockSpec((mm, d), lambda i: (0, 0))
    vec = pl.BlockSpec((1, d), lambda i: (0, 0))
    out_shape = [jax.ShapeDtypeStruct((m, d), F32), jax.ShapeDtypeStruct((mm, d), F32)]
    out_specs = [row, meta]
    if emit_bf16:
        out_shape += [jax.ShapeDtypeStruct((m, d), BF16), jax.ShapeDtypeStruct((mm, d), BF16)]
        out_specs += [row, meta]
    return pl.pallas_call(
        functools.partial(_outproj_ln_kernel, emit_bf16=emit_bf16),
        out_shape=tuple(out_shape),
        grid=(m // LN_TM,),
        in_specs=[
            pl.BlockSpec((cb, LN_TM, LANES), lambda i: (0, i, 0)),
            pl.BlockSpec((cb, mm, LANES), lambda i: (0, 0, 0)),
            pl.BlockSpec((None, d, d), lambda i: (layer, 0, 0), pipeline_mode=pl.Buffered(1)),
            row, meta, vec, vec,
        ],
        out_specs=tuple(out_specs),
        compiler_params=_compiler_params(1),
        name="outproj_ln",
    )(y, ym, w_stack_bf, h, hm, g.reshape(1, d), b.reshape(1, d))


def _na_tables():
    qa = np.arange(NA_GROUP_ROWS)
    ke = np.arange(NA_WIN_ROWS)
    dr_idx = ke[None, :] - NA_ROWS // 2 - qa[:, None] + NA_ROWS - 1
    n_groups = GRID_ROWS // NA_GROUP_ROWS
    oks = []
    for g in range(n_groups):
        r = NA_GROUP_ROWS * g + qa
        r0 = np.clip(r - NA_ROWS // 2, 0, GRID_ROWS - NA_ROWS)
        kr = NA_GROUP_ROWS * g - NA_ROWS // 2 + ke
        oks.append((kr[None, :] >= r0[:, None]) & (kr[None, :] < r0[:, None] + NA_ROWS))
    for g in range(1, n_groups - 1):
        assert np.array_equal(oks[g], oks[1])
    row_ok = np.stack([oks[0], oks[1], oks[n_groups - 1]])
    assert dr_idx[row_ok.any(0)].min() >= 0 and dr_idx[row_ok.any(0)].max() <= 2 * NA_ROWS - 2
    return dr_idx, row_ok


_NA_TABLES = _na_tables()


def _na_build_bias(rpb_ref, toep, comb):
    dr_idx, row_ok = _NA_TABLES
    qc = lax.broadcasted_iota(jnp.int32, (GRID_W, LANES), 0)
    lane = lax.broadcasted_iota(jnp.int32, (GRID_W, LANES), 1)
    kc = lane & (GRID_W - 1)
    c0 = jnp.clip(qc - NA_COLS // 2, 0, GRID_W - NA_COLS)
    col_ok = jnp.logical_and(kc >= c0, kc < c0 + NA_COLS)
    dc_idx = jnp.clip(kc - qc + NA_COLS - 1, 0, LANES - 1)
    for dr in range(2 * NA_ROWS - 1):
        row = jnp.broadcast_to(rpb_ref[dr:dr + 1, :], (GRID_W, LANES))
        toep[dr] = jnp.where(col_ok, LOG2E * jnp.take_along_axis(row, dc_idx, axis=1), NEG_INF)
    masked = jnp.full((GRID_W, LANES), NEG_INF, F32)
    left_half = lane < GRID_W
    meta_tile = jnp.broadcast_to(_meta_lane_mask(), (Q_TILE, META_PAD))
    for kind in range(3):
        for a in range(NA_GROUP_ROWS):
            for ep in range(NA_WIN_ROWS // 2):
                halves = [toep[int(dr_idx[a, e])] if row_ok[kind, a, e] else masked
                          for e in (2 * ep, 2 * ep + 1)]
                comb[kind, a * GRID_W:(a + 1) * GRID_W, ep * LANES:(ep + 1) * LANES] = jnp.where(
                    left_half, halves[0], halves[1])
        comb[kind, :, NA_WIN:NA_KEYS] = meta_tile


def _na_kernel(q_ref, k_ref, v_ref, z_ref, qm_ref, km_ref, vm_ref, zm_ref, rpb_ref,
               y_ref, ym_ref, kpad, vpad, kmp, vmp, toep, comb, *, n_groups):
    b = pl.program_id(1)
    g = pl.program_id(2)
    heads = q_ref.shape[0]

    @pl.when(jnp.logical_and(b == 0, g == 0))
    def _():
        for hh in range(heads):
            _na_build_bias(rpb_ref.at[hh], toep, comb.at[hh])

    @pl.when(g == 0)
    def _():
        zeros = jnp.zeros((NA_PAD, HEAD_DIM), BF16)
        for hh in range(heads):
            for pad, src in ((kpad, k_ref), (vpad, v_ref)):
                pad[hh, 0:NA_PAD] = zeros
                pad[hh, NA_PAD:NA_PAD + SEQ] = src[hh]
                pad[hh, NA_PAD + SEQ:NA_PAD + SEQ + NA_PAD] = zeros
            for pad, src in ((kmp, km_ref), (vmp, vm_ref)):
                pad[hh] = jnp.zeros((META_PAD, HEAD_DIM), BF16)
                pad[hh, 0:N_META] = src[hh]
            s = _dot_nt(qm_ref[hh], kmp[hh]) + _meta_lane_mask()
            p = jnp.exp2(s - jnp.max(s, -1, keepdims=True))
            o = _dot(p.astype(BF16), vmp[hh]) / jnp.sum(p, -1, keepdims=True)
            ym_ref[hh] = (o * _silu(zm_ref[hh].astype(F32))).astype(BF16)

    start = pl.multiple_of(g * Q_TILE, Q_TILE)
    kind = jnp.where(g == 0, 0, jnp.where(g == n_groups - 1, 2, 1))
    scores = []
    for hh in range(heads):
        keys = jnp.concatenate([kpad[hh, pl.ds(start, NA_WIN), :], kmp[hh]], axis=0)
        scores.append(_dot_nt(q_ref[hh], keys) + comb[hh, kind])
    def probs(hh):
        s = scores[hh]
        p = jnp.exp2(s - jnp.max(s, -1, keepdims=True))
        return p.astype(BF16), jnp.sum(p, -1, keepdims=True)

    def finish(hh, p, l):
        vals = jnp.concatenate([vpad[hh, pl.ds(start, NA_WIN), :], vmp[hh]], axis=0)
        o = _dot(p, vals) / l
        y_ref[hh] = (o * _silu(z_ref[hh].astype(F32))).astype(BF16)

    pending = probs(0)
    for hh in range(1, heads):
        nxt = probs(hh)
        finish(hh - 1, *pending)
        pending = nxt
    finish(heads - 1, *pending)


def _na_attention(qkvz, qkvz_m, rpb, bn):
    hp = NA_HEADS_PER_STEP
    n_groups = SEQ // Q_TILE
    n_dr, n_dc = rpb.shape[1:]
    rpb_pad = jnp.pad(rpb.astype(F32), ((0, 0), (0, 16 - n_dr), (0, LANES - n_dc)))
    sec = A_HEADS // hp

    def real(s):
        return pl.BlockSpec((hp, Q_TILE, HEAD_DIM), lambda h, b, g: (s * sec + h, b * n_groups + g, 0))

    def full(s):
        return pl.BlockSpec((hp, SEQ, HEAD_DIM), lambda h, b, g: (s * sec + h, b, 0))

    def meta(s):
        return pl.BlockSpec((hp, N_META, HEAD_DIM), lambda h, b, g: (s * sec + h, b, 0))

    return pl.pallas_call(
        functools.partial(_na_kernel, n_groups=n_groups),
        out_shape=(jax.ShapeDtypeStruct((A_HEADS, bn * SEQ, HEAD_DIM), BF16),
                   jax.ShapeDtypeStruct((A_HEADS, bn * N_META, HEAD_DIM), BF16)),
        grid=(sec, bn, n_groups),
        in_specs=[
            real(0), full(1), full(2), real(3), meta(0), meta(1), meta(2), meta(3),
            pl.BlockSpec((hp, 16, LANES), lambda h, b, g: (h, 0, 0)),
        ],
        out_specs=(real(0), meta(0)),
        scratch_shapes=[
            pltpu.VMEM((hp, SEQ + 2 * NA_PAD, HEAD_DIM), BF16),
            pltpu.VMEM((hp, SEQ + 2 * NA_PAD, HEAD_DIM), BF16),
            pltpu.VMEM((hp, META_PAD, HEAD_DIM), BF16),
            pltpu.VMEM((hp, META_PAD, HEAD_DIM), BF16),
            pltpu.VMEM((2 * NA_ROWS - 1, GRID_W, LANES), F32),
            pltpu.VMEM((hp, 3, Q_TILE, NA_KEYS), F32),
        ],
        compiler_params=_compiler_params(3),
        name="na_attention",
    )(qkvz, qkvz, qkvz, qkvz, qkvz_m, qkvz_m, qkvz_m, qkvz_m, rpb_pad)


def _diff_kernel(slopes_ref, q_ref, k_ref, v_ref, z_ref, qm_ref, km_ref, vm_ref, zm_ref,
                 lq1_ref, lk1_ref, lq2_ref, lk2_ref, subg_ref, y_ref, ym_ref,
                 kaug, kmp, vmp, diag, sbuf, *, lambda_init):
    h = pl.program_id(0)
    b = pl.program_id(1)
    n = pl.program_id(2)
    rows = Q_TILE + N_META
    n_chunk = SEQ // DIFF_KC
    meta_mask = _meta_lane_mask()
    lam = (jnp.exp(jnp.sum(lq1_ref[...] * lk1_ref[...], -1, keepdims=True))
           - jnp.exp(jnp.sum(lq2_ref[...] * lk2_ref[...], -1, keepdims=True)) + lambda_init)
    rate = LOG2E * slopes_ref[h]
    c = jnp.full((1, LANES), rate, F32)
    c1 = c.astype(BF16).astype(F32)
    c2 = (c - c1).astype(BF16).astype(F32)
    c3 = (c - c1 - c2).astype(BF16).astype(F32)

    def rate_piece(lane, first):
        return jnp.where(lane < first + 2, c1, jnp.where(lane < first + 4, c2, c3))

    @pl.when(jnp.logical_and(b == 0, n == 0))
    def _():
        kpos = lax.broadcasted_iota(jnp.int32, (SEQ, LANES), 0)
        lane = lax.broadcasted_iota(jnp.int32, (SEQ, LANES), 1)
        k_lo = (kpos & (DIFF_KC - 1)).astype(F32)
        k_hi = (kpos - (kpos & (DIFF_KC - 1))).astype(F32)
        pair = jnp.where((lane & 1) == 0, k_hi, k_lo)
        feat = jnp.where(lane < 6, -rate_piece(lane, 0), jnp.where(lane < 12, pair, 0.0))
        for j in range(2):
            kaug[j, :, HEAD_DIM:2 * HEAD_DIM] = feat.astype(BF16)
        qi = lax.broadcasted_iota(jnp.int32, (rows, DIFF_KC), 0)
        kj = lax.broadcasted_iota(jnp.int32, (rows, DIFF_KC), 1)
        diag[...] = jnp.where(qi < Q_TILE, -rate * jnp.abs(qi - kj).astype(F32), 0.0)

    @pl.when(n == 0)
    def _():
        vmp[...] = jnp.zeros((META_PAD, 2 * HEAD_DIM), BF16)
        for j in range(2):
            kaug[j, :, 0:HEAD_DIM] = k_ref[j]
            kmp[j] = jnp.zeros((META_PAD, HEAD_DIM), BF16)
            kmp[j, 0:N_META] = km_ref[j]
            vmp[0:N_META, j * HEAD_DIM:(j + 1) * HEAD_DIM] = vm_ref[j]

    row = lax.broadcasted_iota(jnp.int32, (rows, LANES), 0)
    lane = lax.broadcasted_iota(jnp.int32, (rows, LANES), 1)
    pair = jnp.where((lane & 1) == 0, (n * Q_TILE).astype(F32), row.astype(F32))
    qfeat = jnp.where(lane < 6, pair, jnp.where(lane < 12, rate_piece(lane, 6), 0.0))
    qfeat = jnp.where(row < Q_TILE, qfeat, 0.0)
    feat_before = qfeat.astype(BF16)
    feat_after = (-qfeat).astype(BF16)
    feat_diag = jnp.zeros((rows, LANES), BF16)
    chunks = [(n + d) & (n_chunk - 1) for d in range(n_chunk)]

    maps = range(2)
    qs = [jnp.concatenate([q_ref[j], qm_ref[j]], axis=0) for j in maps]
    mx = [None, None]
    for d, c in enumerate(chunks):
        start = pl.multiple_of(c * DIFF_KC, DIFF_KC)
        f = feat_diag if d == 0 else jnp.where(c < n, feat_before, feat_after)
        for j in maps:
            s = _dot_nt(jnp.concatenate([qs[j], f], axis=1), kaug[j, pl.ds(start, DIFF_KC), :])
            if d == 0:
                s = s + diag[...]
            sbuf[j, d] = s
            mx[j] = s if mx[j] is None else jnp.maximum(mx[j], s)
    s_m = [_dot_nt(qs[j], kmp[j]) + meta_mask for j in maps]
    m = [jnp.maximum(jnp.max(mx[j], -1, keepdims=True), jnp.max(s_m[j], -1, keepdims=True))
         for j in maps]
    acc = [None, None]
    o = [None, None]
    for d, c in enumerate(chunks):
        start = pl.multiple_of(c * DIFF_KC, DIFF_KC)
        v_chunk = jnp.concatenate(
            [v_ref[0, pl.ds(start, DIFF_KC), :], v_ref[1, pl.ds(start, DIFF_KC), :]], axis=1)
        for j in maps:
            p = jnp.exp2(sbuf[j, d] - m[j])
            acc[j] = p if acc[j] is None else acc[j] + p
            part = _dot(p.astype(BF16), v_chunk)
            o[j] = part if o[j] is None else o[j] + part
    outs = []
    for j in maps:
        p_m = jnp.exp2(s_m[j] - m[j])
        l = jnp.sum(acc[j], -1, keepdims=True) + jnp.sum(p_m, -1, keepdims=True)
        outs.append((o[j] + _dot(p_m.astype(BF16), vmp[...])) / l)

    o = outs[0] - lam * outs[1]
    o = o * lax.rsqrt(jnp.mean(o * o, -1, keepdims=True) + RMS_EPS)
    o = o * subg_ref[...] * (1.0 - lambda_init)
    z = jnp.concatenate([jnp.concatenate([z_ref[0], z_ref[1]], axis=1),
                         jnp.concatenate([zm_ref[0], zm_ref[1]], axis=1)], axis=0)
    res = (o * _silu(z.astype(F32))).astype(BF16)
    for j in range(2):
        y_ref[j] = res[:Q_TILE, j * HEAD_DIM:(j + 1) * HEAD_DIM]
        ym_ref[j] = res[Q_TILE:, j * HEAD_DIM:(j + 1) * HEAD_DIM]


def _diff_attention(qkvz, qkvz_m, lq1, lk1, lq2, lk2, subg, layer_idx, bn):
    n_blocks = SEQ // Q_TILE
    assert SEQ // DIFF_KC == n_blocks and Q_TILE == DIFF_KC
    lambda_init = 0.8 - 0.6 * math.exp(-0.3 * layer_idx)

    def real(s):
        return pl.BlockSpec((2, Q_TILE, HEAD_DIM), lambda h, b, n: (s * B_HEADS + h, b * n_blocks + n, 0))

    def full(s):
        return pl.BlockSpec((2, SEQ, HEAD_DIM), lambda h, b, n: (s * B_HEADS + h, b, 0))

    def meta(s):
        return pl.BlockSpec((2, N_META, HEAD_DIM), lambda h, b, n: (s * B_HEADS + h, b, 0))

    def vec(width):
        return pl.BlockSpec((1, width), lambda h, b, n: (0, 0))

    rows = Q_TILE + N_META
    return pl.pallas_call(
        functools.partial(_diff_kernel, lambda_init=lambda_init),
        out_shape=(jax.ShapeDtypeStruct((2 * B_HEADS, bn * SEQ, HEAD_DIM), BF16),
                   jax.ShapeDtypeStruct((2 * B_HEADS, bn * N_META, HEAD_DIM), BF16)),
        grid=(B_HEADS, bn, n_blocks),
        in_specs=[
            pl.BlockSpec(memory_space=pltpu.SMEM),
            real(0), full(1), full(2), real(3), meta(0), meta(1), meta(2), meta(3),
            vec(HEAD_DIM), vec(HEAD_DIM), vec(HEAD_DIM), vec(HEAD_DIM), vec(2 * HEAD_DIM),
        ],
        out_specs=(real(0), meta(0)),
        scratch_shapes=[
            pltpu.VMEM((2, SEQ, 2 * HEAD_DIM), BF16),
            pltpu.VMEM((2, META_PAD, HEAD_DIM), BF16),
            pltpu.VMEM((META_PAD, 2 * HEAD_DIM), BF16),
            pltpu.VMEM((rows, DIFF_KC), F32),
            pltpu.VMEM((2, SEQ // DIFF_KC, rows, DIFF_KC), F32),
        ],
        compiler_params=_compiler_params(3),
        name="diff_attention",
    )(_alibi_slopes(B_HEADS), qkvz, qkvz, qkvz, qkvz, qkvz_m, qkvz_m, qkvz_m, qkvz_m,
      lq1.reshape(1, -1), lk1.reshape(1, -1), lq2.reshape(1, -1), lk2.reshape(1, -1),
      subg.reshape(1, -1))


def _swa_tables():
    qi = np.arange(Q_TILE)[:, None]
    kj = np.arange(SWA_WIN)[None, :]
    dist = np.abs(qi + C_WINDOW - kj)
    near = dist <= C_WINDOW
    valid = np.stack([near & (kj >= C_WINDOW), near, near & (kj < C_WINDOW + Q_TILE)])
    return dist.astype(np.float32), valid.astype(np.float32)


def _swa_kernel(slopes_ref, sink_ref, q_ref, k_ref, v_ref, z_ref, qm_ref, km_ref, vm_ref, zm_ref,
                dist_ref, valid_ref, y_ref, ym_ref, kpad, vpad, kmp, vmp, comb, *, n_blocks):
    kh = pl.program_id(0)
    b = pl.program_id(1)
    n = pl.program_id(2)

    def stack_heads(x_ref):
        return jnp.concatenate([x_ref[gq] for gq in range(C_GROUP)], axis=0)

    def meta_sink_tile(gq, rows):
        lane = lax.broadcasted_iota(jnp.int32, (rows, META_PAD), 1)
        sink = LOG2E * sink_ref[kh * C_GROUP + gq]
        return jnp.where(lane < N_META, 0.0, jnp.where(lane == N_META, sink, NEG_INF))

    def gate_and_store(dst_ref, o, z_ref_, rows):
        for gq in range(C_GROUP):
            dst_ref[gq] = (o[gq * rows:(gq + 1) * rows] * _silu(z_ref_[gq].astype(F32))).astype(BF16)

    @pl.when(jnp.logical_and(b == 0, n == 0))
    def _():
        for gq in range(C_GROUP):
            slope = LOG2E * slopes_ref[kh * C_GROUP + gq]
            rows = slice(gq * Q_TILE, (gq + 1) * Q_TILE)
            for c in range(3):
                comb[c, rows, 0:SWA_WIN] = jnp.where(valid_ref[c] > 0.5, -slope * dist_ref[...], NEG_INF)
                comb[c, rows, SWA_WIN:SWA_KEYS] = meta_sink_tile(gq, Q_TILE)

    @pl.when(n == 0)
    def _():
        zeros = jnp.zeros((C_WINDOW, HEAD_DIM), BF16)
        for pad, src in ((kpad, k_ref), (vpad, v_ref)):
            pad[0:C_WINDOW] = zeros
            pad[C_WINDOW:C_WINDOW + SEQ] = src[...]
            pad[C_WINDOW + SEQ:SEQ + 2 * C_WINDOW] = zeros
        for pad, src in ((kmp, km_ref), (vmp, vm_ref)):
            pad[...] = jnp.zeros((META_PAD, HEAD_DIM), BF16)
            pad[0:N_META] = src[...]
        mask = jnp.concatenate([meta_sink_tile(gq, N_META) for gq in range(C_GROUP)], axis=0)
        s = _dot_nt(stack_heads(qm_ref), kmp[...]) + mask
        p = jnp.exp2(s - jnp.max(s, -1, keepdims=True))
        o = _dot(p.astype(BF16), vmp[...]) / jnp.sum(p, -1, keepdims=True)
        gate_and_store(ym_ref, o, zm_ref, N_META)

    start = pl.multiple_of(n * Q_TILE, Q_TILE)
    kind = jnp.where(n == 0, 0, jnp.where(n == n_blocks - 1, 2, 1))
    keys = jnp.concatenate([kpad[pl.ds(start, SWA_WIN), :], kmp[...]], axis=0)
    vals = jnp.concatenate([vpad[pl.ds(start, SWA_WIN), :], vmp[...]], axis=0)
    s_all = _dot_nt(stack_heads(q_ref), keys) + comb[kind]
    for gq in range(C_GROUP):
        s = s_all[gq * Q_TILE:(gq + 1) * Q_TILE]
        p = jnp.exp2(s - jnp.max(s, -1, keepdims=True))
        o = _dot(p.astype(BF16), vals) / jnp.sum(p, -1, keepdims=True)
        y_ref[gq] = (o * _silu(z_ref[gq].astype(F32))).astype(BF16)


def _swa_attention(qkvz, qkvz_m, sink, bn):
    n_blocks = SEQ // Q_TILE
    assert n_blocks >= 2
    k_cb0 = C_HEADS
    v_cb0 = k_cb0 + C_KV_HEADS
    z_grp0 = (v_cb0 + C_KV_HEADS) // C_GROUP
    dist, valid = _swa_tables()

    def q_like(grp0):
        return pl.BlockSpec((C_GROUP, Q_TILE, HEAD_DIM), lambda kh, b, n: (grp0 + kh, b * n_blocks + n, 0))

    def q_like_meta(grp0):
        return pl.BlockSpec((C_GROUP, N_META, HEAD_DIM), lambda kh, b, n: (grp0 + kh, b, 0))

    def kv(cb0, rows):
        return pl.BlockSpec((None, rows, HEAD_DIM), lambda kh, b, n: (cb0 + kh, b, 0))

    smem = pl.BlockSpec(memory_space=pltpu.SMEM)
    return pl.pallas_call(
        functools.partial(_swa_kernel, n_blocks=n_blocks),
        out_shape=(jax.ShapeDtypeStruct((C_HEADS, bn * SEQ, HEAD_DIM), BF16),
                   jax.ShapeDtypeStruct((C_HEADS, bn * N_META, HEAD_DIM), BF16)),
        grid=(C_KV_HEADS, bn, n_blocks),
        in_specs=[
            smem, smem,
            q_like(0), kv(k_cb0, SEQ), kv(v_cb0, SEQ), q_like(z_grp0),
            q_like_meta(0), kv(k_cb0, N_META), kv(v_cb0, N_META), q_like_meta(z_grp0),
            pl.BlockSpec((Q_TILE, SWA_WIN), lambda kh, b, n: (0, 0)),
            pl.BlockSpec((3, Q_TILE, SWA_WIN), lambda kh, b, n: (0, 0, 0)),
        ],
        out_specs=(q_like(0), q_like_meta(0)),
        scratch_shapes=[
            pltpu.VMEM((SEQ + 2 * C_WINDOW, HEAD_DIM), BF16),
            pltpu.VMEM((SEQ + 2 * C_WINDOW, HEAD_DIM), BF16),
            pltpu.VMEM((META_PAD, HEAD_DIM), BF16),
            pltpu.VMEM((META_PAD, HEAD_DIM), BF16),
            pltpu.VMEM((3, C_GROUP * Q_TILE, SWA_KEYS), F32),
        ],
        compiler_params=_compiler_params(3),
        name="swa_attention",
    )(_alibi_slopes(C_HEADS), sink.astype(F32), qkvz, qkvz, qkvz, qkvz,
      qkvz_m, qkvz_m, qkvz_m, qkvz_m, jnp.asarray(dist), jnp.asarray(valid))


def kernel(x, meta_tokens, w_in_a, rpb_a, w_in_b, lam_q1_b, lam_k1_b, lam_q2_b, lam_k2_b,
           subln_g_b, w_in_c, sink_c, w_out, ln_g, ln_b):
    bn, seq, d = x.shape
    assert seq == SEQ and d == D_MODEL
    h = x.reshape(bn * seq, d)
    hm = jnp.tile(meta_tokens.astype(F32), (bn, 1))
    h_bf, hm_bf = h, hm
    w_out_bf = w_out.astype(BF16)
    for i in range(DEPTH):
        kind, j = i % 3, i // 3
        if kind == 0:
            qkvz, qkvz_m = _inproj(h_bf, hm_bf, w_in_a, j, A_HEADS * HEAD_DIM)
            y, ym = _na_attention(qkvz, qkvz_m, rpb_a[j], bn)
        elif kind == 1:
            qkvz, qkvz_m = _inproj(h_bf, hm_bf, w_in_b, j, 2 * B_HEADS * HEAD_DIM)
            y, ym = _diff_attention(qkvz, qkvz_m, lam_q1_b[j], lam_k1_b[j], lam_q2_b[j],
                                    lam_k2_b[j], subln_g_b[j], i, bn)
        else:
            qkvz, qkvz_m = _inproj(h_bf, hm_bf, w_in_c, j, C_HEADS * HEAD_DIM)
            y, ym = _swa_attention(qkvz, qkvz_m, sink_c[j], bn)
        last = i == DEPTH - 1
        outs = _outproj_ln(y, ym, w_out_bf, i, h, hm, ln_g[i], ln_b[i], not last)
        if last:
            h, hm = outs
        else:
            h, hm, h_bf, hm_bf = outs
    return h.reshape(bn, seq, d)
```

```python
import functools
import math

import jax
import jax.numpy as jnp
import numpy as np
from jax import lax
from jax.experimental import pallas as pl
from jax.experimental.pallas import tpu as pltpu

F32 = jnp.float32
BF16 = jnp.bfloat16

D_MODEL = 2048
SEQ = 2048
DEPTH = 4
N_META = 16
GRID_W = 64
GRID_ROWS = SEQ // GRID_W
NA_ROWS = 8
NA_COLS = 16
HEAD_DIM = 128
A_HEADS = 16
B_HEADS = 8
C_HEADS = 16
C_KV_HEADS = 4
C_GROUP = C_HEADS // C_KV_HEADS
C_WINDOW = 128
ALPHA = (2 * DEPTH) ** 0.25
LN_EPS = 1e-5
RMS_EPS = 1e-5
NEG_INF = -1e30
LOG2E = math.log2(math.e)
Q_SCALE_LOG2 = LOG2E * HEAD_DIM ** -0.5

LANES = 128
META_PAD = LANES
VMEM_LIMIT = 56 * 1024 * 1024

PROJ_TM = 1024
PROJ_TN = 1024
LN_TM = 512
LN_SUB = 128
Q_TILE = 256
NA_HEADS_PER_STEP = 4
NA_GROUP_ROWS = Q_TILE // GRID_W
NA_WIN_ROWS = NA_GROUP_ROWS + NA_ROWS
NA_WIN = NA_WIN_ROWS * GRID_W
NA_KEYS = NA_WIN + META_PAD
NA_PAD = (NA_ROWS // 2) * GRID_W
SWA_WIN = Q_TILE + 2 * C_WINDOW
SWA_KEYS = SWA_WIN + META_PAD
DIFF_KC = 256


def _dot_nt(a, b):
    return lax.dot_general(a, b, (((1,), (1,)), ((), ())), preferred_element_type=F32)


def _dot(a, b):
    return jnp.dot(a, b, preferred_element_type=F32)


def _silu(z):
    return z / (1.0 + jnp.exp(-z))


def _meta_lane_mask():
    lane = lax.broadcasted_iota(jnp.int32, (1, META_PAD), 1)
    return jnp.where(lane < N_META, 0.0, NEG_INF).astype(F32)


def _compiler_params(n_axes):
    return pltpu.CompilerParams(
        dimension_semantics=("arbitrary",) * n_axes, vmem_limit_bytes=VMEM_LIMIT)


def _alibi_slopes(n_heads):
    return jnp.asarray(np.exp2(-8.0 * np.arange(1, n_heads + 1) / n_heads), F32)


def _inproj_kernel(x_ref, xm_ref, w_ref, o_ref, om_ref, wbf_ref, *, n_q_tiles):
    j = pl.program_id(0)
    i = pl.program_id(1)
    scale = jnp.where(j < n_q_tiles, Q_SCALE_LOG2, 1.0).astype(F32)

    def put(dst, acc):
        for c in range(PROJ_TN // LANES):
            dst[c] = (acc[:, c * LANES:(c + 1) * LANES] * scale).astype(BF16)

    @pl.when(i == 0)
    def _():
        wbf_ref[...] = w_ref[...].astype(BF16)
        put(om_ref, _dot(xm_ref[...].astype(BF16), wbf_ref[...]))

    put(o_ref, _dot(x_ref[...].astype(BF16), wbf_ref[...]))


def _inproj(x, xm, w_stack, layer, q_cols):
    m, d = x.shape
    mm = xm.shape[0]
    n = w_stack.shape[2]
    tm = min(PROJ_TM, m)
    cb = PROJ_TN // LANES
    assert m % tm == 0 and n % PROJ_TN == 0 and q_cols % PROJ_TN == 0
    kern = functools.partial(_inproj_kernel, n_q_tiles=q_cols // PROJ_TN)
    return pl.pallas_call(
        kern,
        out_shape=(jax.ShapeDtypeStruct((n // LANES, m, LANES), BF16),
                   jax.ShapeDtypeStruct((n // LANES, mm, LANES), BF16)),
        grid=(n // PROJ_TN, m // tm),
        in_specs=[
            pl.BlockSpec((tm, d), lambda j, i: (i, 0)),
            pl.BlockSpec((mm, d), lambda j, i: (0, 0)),
            pl.BlockSpec((None, d, PROJ_TN), lambda j, i: (layer, 0, j)),
        ],
        out_specs=(
            pl.BlockSpec((cb, tm, LANES), lambda j, i: (j, i, 0)),
            pl.BlockSpec((cb, mm, LANES), lambda j, i: (j, 0, 0)),
        ),
        scratch_shapes=[pltpu.VMEM((d, PROJ_TN), BF16)],
        compiler_params=_compiler_params(2),
        name="inproj",
    )(x, xm, w_stack)


def _outproj_ln_kernel(y_ref, ym_ref, w_ref, h_ref, hm_ref, g_ref, b_ref, *out_refs, emit_bf16):
    def layer_norm(y_cb, rows, h):
        y = jnp.concatenate([y_cb[c, rows, :] for c in range(y_cb.shape[0])], axis=1)
        t = ALPHA * h + _dot(y, w_ref[...])
        mu = jnp.mean(t, -1, keepdims=True)
        d = t - mu
        var = jnp.mean(d * d, -1, keepdims=True)
        return d * lax.rsqrt(var + LN_EPS) * g_ref[...] + b_ref[...]

    if emit_bf16:
        o_ref, om_ref, obf_ref, ombf_ref = out_refs
    else:
        o_ref, om_ref = out_refs

    @pl.when(pl.program_id(0) == 0)
    def _():
        r = layer_norm(ym_ref, slice(None), hm_ref[...])
        om_ref[...] = r
        if emit_bf16:
            ombf_ref[...] = r.astype(BF16)

    for t in range(LN_TM // LN_SUB):
        rows = slice(t * LN_SUB, (t + 1) * LN_SUB)
        r = layer_norm(y_ref, rows, h_ref[rows, :])
        o_ref[rows, :] = r
        if emit_bf16:
            obf_ref[rows, :] = r.astype(BF16)


def _outproj_ln(y, ym, w_stack_bf, layer, h, hm, g, b, emit_bf16):
    m, d = h.shape
    mm = hm.shape[0]
    cb = d // LANES
    assert m % LN_TM == 0
    row = pl.BlockSpec((LN_TM, d), lambda i: (i, 0))
    meta = pl.BlockSpec((mm, d), lambda i: (0, 0))
    vec = pl.BlockSpec((1, d), lambda i: (0, 0))
    out_shape = [jax.ShapeDtypeStruct((m, d), F32), jax.ShapeDtypeStruct((mm, d), F32)]
    out_specs = [row, meta]
    if emit_bf16:
        out_shape += [jax.ShapeDtypeStruct((m, d), BF16), jax.ShapeDtypeStruct((mm, d), BF16)]
        out_specs += [row, meta]
    return pl.pallas_call(
        functools.partial(_outproj_ln_kernel, emit_bf16=emit_bf16),
        out_shape=tuple(out_shape),
        grid=(m // LN_TM,),
        in_specs=[
            pl.BlockSpec((cb, LN_TM, LANES), lambda i: (0, i, 0)),
            pl.BlockSpec((cb, mm, LANES), lambda i: (0, 0, 0)),
            pl.BlockSpec((None, d, d), lambda i: (layer, 0, 0), pipeline_mode=pl.Buffered(1)),
            row, meta, vec, vec,
        ],
        out_specs=tuple(out_specs),
        compiler_params=_compiler_params(1),
        name="outproj_ln",
    )(y, ym, w_stack_bf, h, hm, g.reshape(1, d), b.reshape(1, d))


def _na_tables():
    qa = np.arange(NA_GROUP_ROWS)
    ke = np.arange(NA_WIN_ROWS)
    dr_idx = ke[None, :] - NA_ROWS // 2 - qa[:, None] + NA_ROWS - 1
    n_groups = GRID_ROWS // NA_GROUP_ROWS
    oks = []
    for g in range(n_groups):
        r = NA_GROUP_ROWS * g + qa
        r0 = np.clip(r - NA_ROWS // 2, 0, GRID_ROWS - NA_ROWS)
        kr = NA_GROUP_ROWS * g - NA_ROWS // 2 + ke
        oks.append((kr[None, :] >= r0[:, None]) & (kr[None, :] < r0[:, None] + NA_ROWS))
    for g in range(1, n_groups - 1):
        assert np.array_equal(oks[g], oks[1])
    row_ok = np.stack([oks[0], oks[1], oks[n_groups - 1]])
    assert dr_idx[row_ok.any(0)].min() >= 0 and dr_idx[row_ok.any(0)].max() <= 2 * NA_ROWS - 2
    return dr_idx, row_ok


_NA_TABLES = _na_tables()


def _na_build_bias(rpb_ref, toep, comb):
    dr_idx, row_ok = _NA_TABLES
    qc = lax.broadcasted_iota(jnp.int32, (GRID_W, LANES), 0)
    lane = lax.broadcasted_iota(jnp.int32, (GRID_W, LANES), 1)
    kc = lane & (GRID_W - 1)
    c0 = jnp.clip(qc - NA_COLS // 2, 0, GRID_W - NA_COLS)
    col_ok = jnp.logical_and(kc >= c0, kc < c0 + NA_COLS)
    dc_idx = jnp.clip(kc - qc + NA_COLS - 1, 0, LANES - 1)
    for dr in range(2 * NA_ROWS - 1):
        row = jnp.broadcast_to(rpb_ref[dr:dr + 1, :], (GRID_W, LANES))
        toep[dr] = jnp.where(col_ok, LOG2E * jnp.take_along_axis(row, dc_idx, axis=1), NEG_INF)
    masked = jnp.full((GRID_W, LANES), NEG_INF, F32)
    left_half = lane < GRID_W
    meta_tile = jnp.broadcast_to(_meta_lane_mask(), (Q_TILE, META_PAD))
    for kind in range(3):
        for a in range(NA_GROUP_ROWS):
            for ep in range(NA_WIN_ROWS // 2):
                halves = [toep[int(dr_idx[a, e])] if row_ok[kind, a, e] else masked
                          for e in (2 * ep, 2 * ep + 1)]
                comb[kind, a * GRID_W:(a + 1) * GRID_W, ep * LANES:(ep + 1) * LANES] = jnp.where(
                    left_half, halves[0], halves[1])
        comb[kind, :, NA_WIN:NA_KEYS] = meta_tile


def _na_kernel(q_ref, k_ref, v_ref, z_ref, qm_ref, km_ref, vm_ref, zm_ref, rpb_ref,
               y_ref, ym_ref, kpad, vpad, kmp, vmp, toep, comb, *, n_groups):
    b = pl.program_id(1)
    g = pl.program_id(2)
    heads = q_ref.shape[0]

    @pl.when(jnp.logical_and(b == 0, g == 0))
    def _():
        for hh in range(heads):
            _na_build_bias(rpb_ref.at[hh], toep, comb.at[hh])

    @pl.when(g == 0)
    def _():
        zeros = jnp.zeros((NA_PAD, HEAD_DIM), BF16)
        for hh in range(heads):
            for pad, src in ((kpad, k_ref), (vpad, v_ref)):
                pad[hh, 0:NA_PAD] = zeros
                pad[hh, NA_PAD:NA_PAD + SEQ] = src[hh]
                pad[hh, NA_PAD + SEQ:NA_PAD + SEQ + NA_PAD] = zeros
            for pad, src in ((kmp, km_ref), (vmp, vm_ref)):
                pad[hh] = jnp.zeros((META_PAD, HEAD_DIM), BF16)
                pad[hh, 0:N_META] = src[hh]
            s = _dot_nt(qm_ref[hh], kmp[hh]) + _meta_lane_mask()
            p = jnp.exp2(s - jnp.max(s, -1, keepdims=True))
            o = _dot(p.astype(BF16), vmp[hh]) / jnp.sum(p, -1, keepdims=True)
            ym_ref[hh] = (o * _silu(zm_ref[hh].astype(F32))).astype(BF16)

    start = pl.multiple_of(g * Q_TILE, Q_TILE)
    kind = jnp.where(g == 0, 0, jnp.where(g == n_groups - 1, 2, 1))
    scores = []
    for hh in range(heads):
        keys = jnp.concatenate([kpad[hh, pl.ds(start, NA_WIN), :], kmp[hh]], axis=0)
        scores.append(_dot_nt(q_ref[hh], keys) + comb[hh, kind])
    def probs(hh):
        s = scores[hh]
        p = jnp.exp2(s - jnp.max(s, -1, keepdims=True))
        return p.astype(BF16), jnp.sum(p, -1, keepdims=True)

    def finish(hh, p, l):
        vals = jnp.concatenate([vpad[hh, pl.ds(start, NA_WIN), :], vmp[hh]], axis=0)
        o = _dot(p, vals) / l
        y_ref[hh] = (o * _silu(z_ref[hh].astype(F32))).astype(BF16)

    pending = probs(0)
    for hh in range(1, heads):
        nxt = probs(hh)
        finish(hh - 1, *pending)
        pending = nxt
    finish(heads - 1, *pending)


def _na_attention(qkvz, qkvz_m, rpb, bn):
    hp = NA_HEADS_PER_STEP
    n_groups = SEQ // Q_TILE
    n_dr, n_dc = rpb.shape[1:]
    rpb_pad = jnp.pad(rpb.astype(F32), ((0, 0), (0, 16 - n_dr), (0, LANES - n_dc)))
    sec = A_HEADS // hp

    def real(s):
        return pl.BlockSpec((hp, Q_TILE, HEAD_DIM), lambda h, b, g: (s * sec + h, b * n_groups + g, 0))

    def full(s):
        return pl.BlockSpec((hp, SEQ, HEAD_DIM), lambda h, b, g: (s * sec + h, b, 0))

    def meta(s):
        return pl.BlockSpec((hp, N_META, HEAD_DIM), lambda h, b, g: (s * sec + h, b, 0))

    return pl.pallas_call(
        functools.partial(_na_kernel, n_groups=n_groups),
        out_shape=(jax.ShapeDtypeStruct((A_HEADS, bn * SEQ, HEAD_DIM), BF16),
                   jax.ShapeDtypeStruct((A_HEADS, bn * N_META, HEAD_DIM), BF16)),
        grid=(sec, bn, n_groups),
        in_specs=[
            real(0), full(1), full(2), real(3), meta(0), meta(1), meta(2), meta(3),
            pl.BlockSpec((hp, 16, LANES), lambda h, b, g: (h, 0, 0)),
        ],
        out_specs=(real(0), meta(0)),
        scratch_shapes=[
            pltpu.VMEM((hp, SEQ + 2 * NA_PAD, HEAD_DIM), BF16),
            pltpu.VMEM((hp, SEQ + 2 * NA_PAD, HEAD_DIM), BF16),
            pltpu.VMEM((hp, META_PAD, HEAD_DIM), BF16),
            pltpu.VMEM((hp, META_PAD, HEAD_DIM), BF16),
            pltpu.VMEM((2 * NA_ROWS - 1, GRID_W, LANES), F32),
            pltpu.VMEM((hp, 3, Q_TILE, NA_KEYS), F32),
        ],
        compiler_params=_compiler_params(3),
        name="na_attention",
    )(qkvz, qkvz, qkvz, qkvz, qkvz_m, qkvz_m, qkvz_m, qkvz_m, rpb_pad)


def _diff_kernel(slopes_ref, q_ref, k_ref, v_ref, z_ref, qm_ref, km_ref, vm_ref, zm_ref,
                 lq1_ref, lk1_ref, lq2_ref, lk2_ref, subg_ref, y_ref, ym_ref,
                 kaug, kmp, vmp, diag, sbuf, smeta, mbuf, *, lambda_init, n_blocks, n_steps):
    h = pl.program_id(0)
    t = pl.program_id(1)
    rows = Q_TILE + N_META
    n_chunk = SEQ // DIFF_KC
    n_score = jnp.minimum(t, n_steps - 1) % n_blocks
    n_value = jnp.maximum(t - 1, 0) % n_blocks
    meta_mask = _meta_lane_mask()
    lam = (jnp.exp(jnp.sum(lq1_ref[...] * lk1_ref[...], -1, keepdims=True))
           - jnp.exp(jnp.sum(lq2_ref[...] * lk2_ref[...], -1, keepdims=True)) + lambda_init)
    rate = LOG2E * slopes_ref[h]
    c = jnp.full((1, LANES), rate, F32)
    c1 = c.astype(BF16).astype(F32)
    c2 = (c - c1).astype(BF16).astype(F32)
    c3 = (c - c1 - c2).astype(BF16).astype(F32)

    def rate_piece(lane, first):
        return jnp.where(lane < first + 2, c1, jnp.where(lane < first + 4, c2, c3))

    @pl.when(jnp.logical_and(h == 0, t == 0))
    def _():
        sbuf[...] = jnp.zeros(sbuf.shape, F32)
        smeta[...] = jnp.zeros(smeta.shape, F32)
        mbuf[...] = jnp.zeros(mbuf.shape, F32)

    @pl.when(t == 0)
    def _():
        kpos = lax.broadcasted_iota(jnp.int32, (SEQ, LANES), 0)
        lane = lax.broadcasted_iota(jnp.int32, (SEQ, LANES), 1)
        k_lo = (kpos & (DIFF_KC - 1)).astype(F32)
        k_hi = (kpos - (kpos & (DIFF_KC - 1))).astype(F32)
        pair = jnp.where((lane & 1) == 0, k_hi, k_lo)
        feat = jnp.where(lane < 6, -rate_piece(lane, 0), jnp.where(lane < 12, pair, 0.0))
        for j in range(2):
            kaug[j, :, HEAD_DIM:2 * HEAD_DIM] = feat.astype(BF16)
        qi = lax.broadcasted_iota(jnp.int32, (rows, DIFF_KC), 0)
        kj = lax.broadcasted_iota(jnp.int32, (rows, DIFF_KC), 1)
        diag[...] = jnp.where(qi < Q_TILE, -rate * jnp.abs(qi - kj).astype(F32), 0.0)

    @pl.when(n_score == 0)
    def _():
        for j in range(2):
            kaug[j, :, 0:HEAD_DIM] = k_ref[j]
            kmp[j] = jnp.zeros((META_PAD, HEAD_DIM), BF16)
            kmp[j, 0:N_META] = km_ref[j]

    @pl.when(n_value == 0)
    def _():
        vmp[...] = jnp.zeros((META_PAD, 2 * HEAD_DIM), BF16)
        for j in range(2):
            vmp[0:N_META, j * HEAD_DIM:(j + 1) * HEAD_DIM] = vm_ref[j]

    maps = range(2)

    def score_block(slot):
        row = lax.broadcasted_iota(jnp.int32, (rows, LANES), 0)
        lane = lax.broadcasted_iota(jnp.int32, (rows, LANES), 1)
        pair = jnp.where((lane & 1) == 0, (n_score * Q_TILE).astype(F32), row.astype(F32))
        qfeat = jnp.where(lane < 6, pair, jnp.where(lane < 12, rate_piece(lane, 6), 0.0))
        qfeat = jnp.where(row < Q_TILE, qfeat, 0.0)
        feat_before = qfeat.astype(BF16)
        feat_after = (-qfeat).astype(BF16)
        feat_diag = jnp.zeros((rows, LANES), BF16)

        qs = [jnp.concatenate([q_ref[j], qm_ref[j]], axis=0) for j in maps]
        mx = [None, None]
        for d in range(n_chunk):
            ck = (n_score + d) & (n_chunk - 1)
            start = pl.multiple_of(ck * DIFF_KC, DIFF_KC)
            f = feat_diag if d == 0 else jnp.where(ck < n_score, feat_before, feat_after)
            for j in maps:
                s = _dot_nt(jnp.concatenate([qs[j], f], axis=1), kaug[j, pl.ds(start, DIFF_KC), :])
                if d == 0:
                    s = s + diag[...]
                sbuf[slot, j, d] = s
                mx[j] = s if mx[j] is None else jnp.maximum(mx[j], s)
        for j in maps:
            s_m = _dot_nt(qs[j], kmp[j]) + meta_mask
            smeta[slot, j] = s_m
            mbuf[slot, j] = jnp.maximum(jnp.max(mx[j], -1, keepdims=True),
                                        jnp.max(s_m, -1, keepdims=True))

    def finish_block(slot):
        m = [mbuf[slot, j] for j in maps]
        acc = [None, None]
        o = [None, None]
        for d in range(n_chunk):
            cv = (n_value + d) & (n_chunk - 1)
            start = pl.multiple_of(cv * DIFF_KC, DIFF_KC)
            v_chunk = jnp.concatenate(
                [v_ref[0, pl.ds(start, DIFF_KC), :], v_ref[1, pl.ds(start, DIFF_KC), :]], axis=1)
            for j in maps:
                p = jnp.exp2(sbuf[slot, j, d] - m[j])
                acc[j] = p if acc[j] is None else acc[j] + p
                part = _dot(p.astype(BF16), v_chunk)
                o[j] = part if o[j] is None else o[j] + part
        outs = []
        for j in maps:
            p_m = jnp.exp2(smeta[slot, j] - m[j])
            l = jnp.sum(acc[j], -1, keepdims=True) + jnp.sum(p_m, -1, keepdims=True)
            outs.append((o[j] + _dot(p_m.astype(BF16), vmp[...])) / l)

        o = outs[0] - lam * outs[1]
        o = o * lax.rsqrt(jnp.mean(o * o, -1, keepdims=True) + RMS_EPS)
        o = o * subg_ref[...] * (1.0 - lambda_init)
        z = jnp.concatenate([jnp.concatenate([z_ref[0], z_ref[1]], axis=1),
                             jnp.concatenate([zm_ref[0], zm_ref[1]], axis=1)], axis=0)
        res = (o * _silu(z.astype(F32))).astype(BF16)
        for j in maps:
            y_ref[j] = res[:Q_TILE, j * HEAD_DIM:(j + 1) * HEAD_DIM]
            ym_ref[j] = res[Q_TILE:, j * HEAD_DIM:(j + 1) * HEAD_DIM]

    for parity in range(2):
        @pl.when(t % 2 == parity)
        def _():
            score_block(parity)
            finish_block(1 - parity)


def _diff_attention(qkvz, qkvz_m, lq1, lk1, lq2, lk2, subg, layer_idx, bn):
    n_blocks = SEQ // Q_TILE
    n_steps = bn * n_blocks
    assert SEQ // DIFF_KC == n_blocks and Q_TILE == DIFF_KC
    lambda_init = 0.8 - 0.6 * math.exp(-0.3 * layer_idx)

    def scored(t):
        return jnp.minimum(t, n_steps - 1)

    def finished(t):
        return jnp.maximum(t - 1, 0)

    def spec(s, rows, block_of):
        return pl.BlockSpec((2, rows, HEAD_DIM), lambda h, t: (s * B_HEADS + h, block_of(t), 0))

    def vec(width):
        return pl.BlockSpec((1, width), lambda h, t: (0, 0))

    rows = Q_TILE + N_META
    return pl.pallas_call(
        functools.partial(_diff_kernel, lambda_init=lambda_init, n_blocks=n_blocks, n_steps=n_steps),
        out_shape=(jax.ShapeDtypeStruct((2 * B_HEADS, bn * SEQ, HEAD_DIM), BF16),
                   jax.ShapeDtypeStruct((2 * B_HEADS, bn * N_META, HEAD_DIM), BF16)),
        grid=(B_HEADS, n_steps + 1),
        in_specs=[
            pl.BlockSpec(memory_space=pltpu.SMEM),
            spec(0, Q_TILE, scored),
            spec(1, SEQ, lambda t: scored(t) // n_blocks),
            spec(2, SEQ, lambda t: finished(t) // n_blocks),
            spec(3, Q_TILE, finished),
            spec(0, N_META, lambda t: scored(t) // n_blocks),
            spec(1, N_META, lambda t: scored(t) // n_blocks),
            spec(2, N_META, lambda t: finished(t) // n_blocks),
            spec(3, N_META, lambda t: finished(t) // n_blocks),
            vec(HEAD_DIM), vec(HEAD_DIM), vec(HEAD_DIM), vec(HEAD_DIM), vec(2 * HEAD_DIM),
        ],
        out_specs=(spec(0, Q_TILE, finished), spec(0, N_META, lambda t: finished(t) // n_blocks)),
        scratch_shapes=[
            pltpu.VMEM((2, SEQ, 2 * HEAD_DIM), BF16),
            pltpu.VMEM((2, META_PAD, HEAD_DIM), BF16),
            pltpu.VMEM((META_PAD, 2 * HEAD_DIM), BF16),
            pltpu.VMEM((rows, DIFF_KC), F32),
            pltpu.VMEM((2, 2, SEQ // DIFF_KC, rows, DIFF_KC), F32),
            pltpu.VMEM((2, 2, rows, META_PAD), F32),
            pltpu.VMEM((2, 2, rows, 1), F32),
        ],
        compiler_params=_compiler_params(2),
        name="diff_attention",
    )(_alibi_slopes(B_HEADS), qkvz, qkvz, qkvz, qkvz, qkvz_m, qkvz_m, qkvz_m, qkvz_m,
      lq1.reshape(1, -1), lk1.reshape(1, -1), lq2.reshape(1, -1), lk2.reshape(1, -1),
      subg.reshape(1, -1))


def _swa_tables():
    qi = np.arange(Q_TILE)[:, None]
    kj = np.arange(SWA_WIN)[None, :]
    dist = np.abs(qi + C_WINDOW - kj)
    near = dist <= C_WINDOW
    valid = np.stack([near & (kj >= C_WINDOW), near, near & (kj < C_WINDOW + Q_TILE)])
    return dist.astype(np.float32), valid.astype(np.float32)


def _swa_kernel(slopes_ref, sink_ref, q_ref, k_ref, v_ref, z_ref, qm_ref, km_ref, vm_ref, zm_ref,
                dist_ref, valid_ref, y_ref, ym_ref, kpad, vpad, kmp, vmp, comb, *, n_blocks):
    kh = pl.program_id(0)
    b = pl.program_id(1)
    n = pl.program_id(2)

    def stack_heads(x_ref):
        return jnp.concatenate([x_ref[gq] for gq in range(C_GROUP)], axis=0)

    def meta_sink_tile(gq, rows):
        lane = lax.broadcasted_iota(jnp.int32, (rows, META_PAD), 1)
        sink = LOG2E * sink_ref[kh * C_GROUP + gq]
        return jnp.where(lane < N_META, 0.0, jnp.where(lane == N_META, sink, NEG_INF))

    def gate_and_store(dst_ref, o, z_ref_, rows):
        for gq in range(C_GROUP):
            dst_ref[gq] = (o[gq * rows:(gq + 1) * rows] * _silu(z_ref_[gq].astype(F32))).astype(BF16)

    @pl.when(jnp.logical_and(b == 0, n == 0))
    def _():
        for gq in range(C_GROUP):
            slope = LOG2E * slopes_ref[kh * C_GROUP + gq]
            rows = slice(gq * Q_TILE, (gq + 1) * Q_TILE)
            for c in range(3):
                comb[c, rows, 0:SWA_WIN] = jnp.where(valid_ref[c] > 0.5, -slope * dist_ref[...], NEG_INF)
                comb[c, rows, SWA_WIN:SWA_KEYS] = meta_sink_tile(gq, Q_TILE)

    @pl.when(n == 0)
    def _():
        zeros = jnp.zeros((C_WINDOW, HEAD_DIM), BF16)
        for pad, src in ((kpad, k_ref), (vpad, v_ref)):
            pad[0:C_WINDOW] = zeros
            pad[C_WINDOW:C_WINDOW + SEQ] = src[...]
            pad[C_WINDOW + SEQ:SEQ + 2 * C_WINDOW] = zeros
        for pad, src in ((kmp, km_ref), (vmp, vm_ref)):
            pad[...] = jnp.zeros((META_PAD, HEAD_DIM), BF16)
            pad[0:N_META] = src[...]
        mask = jnp.concatenate([meta_sink_tile(gq, N_META) for gq in range(C_GROUP)], axis=0)
        s = _dot_nt(stack_heads(qm_ref), kmp[...]) + mask
        p = jnp.exp2(s - jnp.max(s, -1, keepdims=True))
        o = _dot(p.astype(BF16), vmp[...]) / jnp.sum(p, -1, keepdims=True)
        gate_and_store(ym_ref, o, zm_ref, N_META)

    start = pl.multiple_of(n * Q_TILE, Q_TILE)
    kind = jnp.where(n == 0, 0, jnp.where(n == n_blocks - 1, 2, 1))
    keys = jnp.concatenate([kpad[pl.ds(start, SWA_WIN), :], kmp[...]], axis=0)
    vals = jnp.concatenate([vpad[pl.ds(start, SWA_WIN), :], vmp[...]], axis=0)
    s_all = _dot_nt(stack_heads(q_ref), keys) + comb[kind]
    for gq in range(C_GROUP):
        s = s_all[gq * Q_TILE:(gq + 1) * Q_TILE]
        p = jnp.exp2(s - jnp.max(s, -1, keepdims=True))
        o = _dot(p.astype(BF16), vals) / jnp.sum(p, -1, keepdims=True)
        y_ref[gq] = (o * _silu(z_ref[gq].astype(F32))).astype(BF16)


def _swa_attention(qkvz, qkvz_m, sink, bn):
    n_blocks = SEQ // Q_TILE
    assert n_blocks >= 2
    k_cb0 = C_HEADS
    v_cb0 = k_cb0 + C_KV_HEADS
    z_grp0 = (v_cb0 + C_KV_HEADS) // C_GROUP
    dist, valid = _swa_tables()

    def q_like(grp0):
        return pl.BlockSpec((C_GROUP, Q_TILE, HEAD_DIM), lambda kh, b, n: (grp0 + kh, b * n_blocks + n, 0))

    def q_like_meta(grp0):
        return pl.BlockSpec((C_GROUP, N_META, HEAD_DIM), lambda kh, b, n: (grp0 + kh, b, 0))

    def kv(cb0, rows):
        return pl.BlockSpec((None, rows, HEAD_DIM), lambda kh, b, n: (cb0 + kh, b, 0))

    smem = pl.BlockSpec(memory_space=pltpu.SMEM)
    return pl.pallas_call(
        functools.partial(_swa_kernel, n_blocks=n_blocks),
        out_shape=(jax.ShapeDtypeStruct((C_HEADS, bn * SEQ, HEAD_DIM), BF16),
                   jax.ShapeDtypeStruct((C_HEADS, bn * N_META, HEAD_DIM), BF16)),
        grid=(C_KV_HEADS, bn, n_blocks),
        in_specs=[
            smem, smem,
            q_like(0), kv(k_cb0, SEQ), kv(v_cb0, SEQ), q_like(z_grp0),
            q_like_meta(0), kv(k_cb0, N_META), kv(v_cb0, N_META), q_like_meta(z_grp0),
            pl.BlockSpec((Q_TILE, SWA_WIN), lambda kh, b, n: (0, 0)),
            pl.BlockSpec((3, Q_TILE, SWA_WIN), lambda kh, b, n: (0, 0, 0)),
        ],
        out_specs=(q_like(0), q_like_meta(0)),
        scratch_shapes=[
            pltpu.VMEM((SEQ + 2 * C_WINDOW, HEAD_DIM), BF16),
            pltpu.VMEM((SEQ + 2 * C_WINDOW, HEAD_DIM), BF16),
            pltpu.VMEM((META_PAD, HEAD_DIM), BF16),
            pltpu.VMEM((META_PAD, HEAD_DIM), BF16),
            pltpu.VMEM((3, C_GROUP * Q_TILE, SWA_KEYS), F32),
        ],
        compiler_params=_compiler_params(3),
        name="swa_attention",
    )(_alibi_slopes(C_HEADS), sink.astype(F32), qkvz, qkvz, qkvz, qkvz,
      qkvz_m, qkvz_m, qkvz_m, qkvz_m, jnp.asarray(dist), jnp.asarray(valid))


def kernel(x, meta_tokens, w_in_a, rpb_a, w_in_b, lam_q1_b, lam_k1_b, lam_q2_b, lam_k2_b,
           subln_g_b, w_in_c, sink_c, w_out, ln_g, ln_b):
    bn, seq, d = x.shape
    assert seq == SEQ and d == D_MODEL
    h = x.reshape(bn * seq, d)
    hm = jnp.tile(meta_tokens.astype(F32), (bn, 1))
    h_bf, hm_bf = h, hm
    w_out_bf = w_out.astype(BF16)
    for i in range(DEPTH):
        kind, j = i % 3, i // 3
        if kind == 0:
            qkvz, qkvz_m = _inproj(h_bf, hm_bf, w_in_a, j, A_HEADS * HEAD_DIM)
            y, ym = _na_attention(qkvz, qkvz_m, rpb_a[j], bn)
        elif kind == 1:
            qkvz, qkvz_m = _inproj(h_bf, hm_bf, w_in_b, j, 2 * B_HEADS * HEAD_DIM)
            y, ym = _diff_attention(qkvz, qkvz_m, lam_q1_b[j], lam_k1_b[j], lam_q2_b[j],
                                    lam_k2_b[j], subln_g_b[j], i, bn)
        else:
            qkvz, qkvz_m = _inproj(h_bf, hm_bf, w_in_c, j, C_HEADS * HEAD_DIM)
            y, ym = _swa_attention(qkvz, qkvz_m, sink_c[j], bn)
        last = i == DEPTH - 1
        outs = _outproj_ln(y, ym, w_out_bf, i, h, hm, ln_g[i], ln_b[i], not last)
        if last:
            h, hm = outs
        else:
            h, hm, h_bf, hm_bf = outs
    return h.reshape(bn, seq, d)
```

```python
import functools
import math

import jax
import jax.numpy as jnp
import numpy as np
from jax import lax
from jax.experimental import pallas as pl
from jax.experimental.pallas import tpu as pltpu

F32 = jnp.float32
BF16 = jnp.bfloat16

D_MODEL = 2048
SEQ = 2048
DEPTH = 4
N_META = 16
GRID_W = 64
GRID_ROWS = SEQ // GRID_W
NA_ROWS = 8
NA_COLS = 16
HEAD_DIM = 128
A_HEADS = 16
B_HEADS = 8
C_HEADS = 16
C_KV_HEADS = 4
C_GROUP = C_HEADS // C_KV_HEADS
C_WINDOW = 128
ALPHA = (2 * DEPTH) ** 0.25
LN_EPS = 1e-5
RMS_EPS = 1e-5
NEG_INF = -1e30
LOG2E = math.log2(math.e)
Q_SCALE_LOG2 = LOG2E * HEAD_DIM ** -0.5

LANES = 128
META_PAD = LANES
VMEM_LIMIT = 56 * 1024 * 1024

PROJ_TM = 1024
PROJ_TN = 1024
LN_TM = 512
LN_SUB = 128
Q_TILE = 256
TILES_PER_STEP = 2
Q_STEP = TILES_PER_STEP * Q_TILE
NA_HEADS_PER_STEP = 4
NA_GROUP_ROWS = Q_TILE // GRID_W
NA_WIN_ROWS = NA_GROUP_ROWS + NA_ROWS
NA_WIN = NA_WIN_ROWS * GRID_W
NA_KEYS = NA_WIN + META_PAD
NA_PAD = (NA_ROWS // 2) * GRID_W
SWA_WIN = Q_TILE + 2 * C_WINDOW
SWA_KEYS = SWA_WIN + META_PAD
DIFF_KC = 256


def _dot_nt(a, b):
    return lax.dot_general(a, b, (((1,), (1,)), ((), ())), preferred_element_type=F32)


def _dot(a, b):
    return jnp.dot(a, b, preferred_element_type=F32)


def _silu(z):
    return z / (1.0 + jnp.exp(-z))


def _meta_lane_mask():
    lane = lax.broadcasted_iota(jnp.int32, (1, META_PAD), 1)
    return jnp.where(lane < N_META, 0.0, NEG_INF).astype(F32)


def _compiler_params(n_axes):
    return pltpu.CompilerParams(
        dimension_semantics=("arbitrary",) * n_axes, vmem_limit_bytes=VMEM_LIMIT)


def _alibi_slopes(n_heads):
    return jnp.asarray(np.exp2(-8.0 * np.arange(1, n_heads + 1) / n_heads), F32)


def _inproj_kernel(x_ref, xm_ref, w_ref, o_ref, om_ref, wbf_ref, *, n_q_tiles):
    j = pl.program_id(0)
    i = pl.program_id(1)
    scale = jnp.where(j < n_q_tiles, Q_SCALE_LOG2, 1.0).astype(F32)

    def put(dst, acc):
        for c in range(PROJ_TN // LANES):
            dst[c] = (acc[:, c * LANES:(c + 1) * LANES] * scale).astype(BF16)

    @pl.when(i == 0)
    def _():
        wbf_ref[...] = w_ref[...].astype(BF16)
        put(om_ref, _dot(xm_ref[...].astype(BF16), wbf_ref[...]))

    put(o_ref, _dot(x_ref[...].astype(BF16), wbf_ref[...]))


def _inproj(x, xm, w_stack, layer, q_cols):
    m, d = x.shape
    mm = xm.shape[0]
    n = w_stack.shape[2]
    tm = min(PROJ_TM, m)
    cb = PROJ_TN // LANES
    assert m % tm == 0 and n % PROJ_TN == 0 and q_cols % PROJ_TN == 0
    kern = functools.partial(_inproj_kernel, n_q_tiles=q_cols // PROJ_TN)
    return pl.pallas_call(
        kern,
        out_shape=(jax.ShapeDtypeStruct((n // LANES, m, LANES), BF16),
                   jax.ShapeDtypeStruct((n // LANES, mm, LANES), BF16)),
        grid=(n // PROJ_TN, m // tm),
        in_specs=[
            pl.BlockSpec((tm, d), lambda j, i: (i, 0)),
            pl.BlockSpec((mm, d), lambda j, i: (0, 0)),
            pl.BlockSpec((None, d, PROJ_TN), lambda j, i: (layer, 0, j)),
        ],
        out_specs=(
            pl.BlockSpec((cb, tm, LANES), lambda j, i: (j, i, 0)),
            pl.BlockSpec((cb, mm, LANES), lambda j, i: (j, 0, 0)),
        ),
        scratch_shapes=[pltpu.VMEM((d, PROJ_TN), BF16)],
        compiler_params=_compiler_params(2),
        name="inproj",
    )(x, xm, w_stack)


def _outproj_ln_kernel(y_ref, ym_ref, w_ref, h_ref, hm_ref, g_ref, b_ref, *out_refs, emit_bf16):
    def layer_norm(y_cb, rows, h):
        y = jnp.concatenate([y_cb[c, rows, :] for c in range(y_cb.shape[0])], axis=1)
        t = ALPHA * h + _dot(y, w_ref[...])
        mu = jnp.mean(t, -1, keepdims=True)
        d = t - mu
        var = jnp.mean(d * d, -1, keepdims=True)
        return d * lax.rsqrt(var + LN_EPS) * g_ref[...] + b_ref[...]

    if emit_bf16:
        o_ref, om_ref, obf_ref, ombf_ref = out_refs
    else:
        o_ref, om_ref = out_refs

    @pl.when(pl.program_id(0) == 0)
    def _():
        r = layer_norm(ym_ref, slice(None), hm_ref[...])
        om_ref[...] = r
        if emit_bf16:
            ombf_ref[...] = r.astype(BF16)

    for t in range(LN_TM // LN_SUB):
        rows = slice(t * LN_SUB, (t + 1) * LN_SUB)
        r = layer_norm(y_ref, rows, h_ref[rows, :])
        o_ref[rows, :] = r
        if emit_bf16:
            obf_ref[rows, :] = r.astype(BF16)


def _outproj_ln(y, ym, w_stack_bf, layer, h, hm, g, b, emit_bf16):
    m, d = h.shape
    mm = hm.shape[0]
    cb = d // LANES
    assert m % LN_TM == 0
    row = pl.BlockSpec((LN_TM, d), lambda i: (i, 0))
    meta = pl.BlockSpec((mm, d), lambda i: (0, 0))
    vec = pl.BlockSpec((1, d), lambda i: (0, 0))
    out_shape = [jax.ShapeDtypeStruct((m, d), F32), jax.ShapeDtypeStruct((mm, d), F32)]
    out_specs = [row, meta]
    if emit_bf16:
        out_shape += [jax.ShapeDtypeStruct((m, d), BF16), jax.ShapeDtypeStruct((mm, d), BF16)]
        out_specs += [row, meta]
    return pl.pallas_call(
        functools.partial(_outproj_ln_kernel, emit_bf16=emit_bf16),
        out_shape=tuple(out_shape),
        grid=(m // LN_TM,),
        in_specs=[
            pl.BlockSpec((cb, LN_TM, LANES), lambda i: (0, i, 0)),
            pl.BlockSpec((cb, mm, LANES), lambda i: (0, 0, 0)),
            pl.BlockSpec((None, d, d), lambda i: (layer, 0, 0), pipeline_mode=pl.Buffered(1)),
            row, meta, vec, vec,
        ],
        out_specs=tuple(out_specs),
        compiler_params=_compiler_params(1),
        name="outproj_ln",
    )(y, ym, w_stack_bf, h, hm, g.reshape(1, d), b.reshape(1, d))


def _na_tables():
    qa = np.arange(NA_GROUP_ROWS)
    ke = np.arange(NA_WIN_ROWS)
    dr_idx = ke[None, :] - NA_ROWS // 2 - qa[:, None] + NA_ROWS - 1
    n_groups = GRID_ROWS // NA_GROUP_ROWS
    oks = []
    for g in range(n_groups):
        r = NA_GROUP_ROWS * g + qa
        r0 = np.clip(r - NA_ROWS // 2, 0, GRID_ROWS - NA_ROWS)
        kr = NA_GROUP_ROWS * g - NA_ROWS // 2 + ke
        oks.append((kr[None, :] >= r0[:, None]) & (kr[None, :] < r0[:, None] + NA_ROWS))
    for g in range(1, n_groups - 1):
        assert np.array_equal(oks[g], oks[1])
    row_ok = np.stack([oks[0], oks[1], oks[n_groups - 1]])
    assert dr_idx[row_ok.any(0)].min() >= 0 and dr_idx[row_ok.any(0)].max() <= 2 * NA_ROWS - 2
    return dr_idx, row_ok


_NA_TABLES = _na_tables()


def _na_build_bias(rpb_ref, toep, comb):
    dr_idx, row_ok = _NA_TABLES
    qc = lax.broadcasted_iota(jnp.int32, (GRID_W, LANES), 0)
    lane = lax.broadcasted_iota(jnp.int32, (GRID_W, LANES), 1)
    kc = lane & (GRID_W - 1)
    c0 = jnp.clip(qc - NA_COLS // 2, 0, GRID_W - NA_COLS)
    col_ok = jnp.logical_and(kc >= c0, kc < c0 + NA_COLS)
    dc_idx = jnp.clip(kc - qc + NA_COLS - 1, 0, LANES - 1)
    for dr in range(2 * NA_ROWS - 1):
        row = jnp.broadcast_to(rpb_ref[dr:dr + 1, :], (GRID_W, LANES))
        toep[dr] = jnp.where(col_ok, LOG2E * jnp.take_along_axis(row, dc_idx, axis=1), NEG_INF)
    masked = jnp.full((GRID_W, LANES), NEG_INF, F32)
    left_half = lane < GRID_W
    meta_tile = jnp.broadcast_to(_meta_lane_mask(), (Q_TILE, META_PAD))
    for kind in range(3):
        for a in range(NA_GROUP_ROWS):
            for ep in range(NA_WIN_ROWS // 2):
                halves = [toep[int(dr_idx[a, e])] if row_ok[kind, a, e] else masked
                          for e in (2 * ep, 2 * ep + 1)]
                comb[kind, a * GRID_W:(a + 1) * GRID_W, ep * LANES:(ep + 1) * LANES] = jnp.where(
                    left_half, halves[0], halves[1])
        comb[kind, :, NA_WIN:NA_KEYS] = meta_tile


def _na_kernel(q_ref, k_ref, v_ref, z_ref, qm_ref, km_ref, vm_ref, zm_ref, rpb_ref,
               y_ref, ym_ref, kpad, vpad, kmp, vmp, toep, comb, *, n_groups):
    b = pl.program_id(1)
    gs = pl.program_id(2)
    heads = q_ref.shape[0]

    @pl.when(jnp.logical_and(b == 0, gs == 0))
    def _():
        for hh in range(heads):
            _na_build_bias(rpb_ref.at[hh], toep, comb.at[hh])

    @pl.when(gs == 0)
    def _():
        zeros = jnp.zeros((NA_PAD, HEAD_DIM), BF16)
        for hh in range(heads):
            for pad, src in ((kpad, k_ref), (vpad, v_ref)):
                pad[hh, 0:NA_PAD] = zeros
                pad[hh, NA_PAD:NA_PAD + SEQ] = src[hh]
                pad[hh, NA_PAD + SEQ:NA_PAD + SEQ + NA_PAD] = zeros
            for pad, src in ((kmp, km_ref), (vmp, vm_ref)):
                pad[hh] = jnp.zeros((META_PAD, HEAD_DIM), BF16)
                pad[hh, 0:N_META] = src[hh]
            s = _dot_nt(qm_ref[hh], kmp[hh]) + _meta_lane_mask()
            p = jnp.exp2(s - jnp.max(s, -1, keepdims=True))
            o = _dot(p.astype(BF16), vmp[hh]) / jnp.sum(p, -1, keepdims=True)
            ym_ref[hh] = (o * _silu(zm_ref[hh].astype(F32))).astype(BF16)

    chains = []
    for ti in range(TILES_PER_STEP):
        g = gs * TILES_PER_STEP + ti
        start = pl.multiple_of(g * Q_TILE, Q_TILE)
        kind = jnp.where(g == 0, 0, jnp.where(g == n_groups - 1, 2, 1))
        rows = slice(ti * Q_TILE, (ti + 1) * Q_TILE)
        for hh in range(heads):
            keys = jnp.concatenate([kpad[hh, pl.ds(start, NA_WIN), :], kmp[hh]], axis=0)
            chains.append((hh, rows, start, _dot_nt(q_ref[hh, rows, :], keys) + comb[hh, kind]))
    for hh, rows, start, s in chains:
        p = jnp.exp2(s - jnp.max(s, -1, keepdims=True))
        vals = jnp.concatenate([vpad[hh, pl.ds(start, NA_WIN), :], vmp[hh]], axis=0)
        o = _dot(p.astype(BF16), vals) / jnp.sum(p, -1, keepdims=True)
        y_ref[hh, rows, :] = (o * _silu(z_ref[hh, rows, :].astype(F32))).astype(BF16)


def _na_attention(qkvz, qkvz_m, rpb, bn):
    hp = NA_HEADS_PER_STEP
    n_groups = SEQ // Q_TILE
    n_steps = n_groups // TILES_PER_STEP
    n_dr, n_dc = rpb.shape[1:]
    rpb_pad = jnp.pad(rpb.astype(F32), ((0, 0), (0, 16 - n_dr), (0, LANES - n_dc)))
    sec = A_HEADS // hp

    def real(s):
        return pl.BlockSpec((hp, Q_STEP, HEAD_DIM), lambda h, b, g: (s * sec + h, b * n_steps + g, 0))

    def full(s):
        return pl.BlockSpec((hp, SEQ, HEAD_DIM), lambda h, b, g: (s * sec + h, b, 0))

    def meta(s):
        return pl.BlockSpec((hp, N_META, HEAD_DIM), lambda h, b, g: (s * sec + h, b, 0))

    return pl.pallas_call(
        functools.partial(_na_kernel, n_groups=n_groups),
        out_shape=(jax.ShapeDtypeStruct((A_HEADS, bn * SEQ, HEAD_DIM), BF16),
                   jax.ShapeDtypeStruct((A_HEADS, bn * N_META, HEAD_DIM), BF16)),
        grid=(sec, bn, n_steps),
        in_specs=[
            real(0), full(1), full(2), real(3), meta(0), meta(1), meta(2), meta(3),
            pl.BlockSpec((hp, 16, LANES), lambda h, b, g: (h, 0, 0)),
        ],
        out_specs=(real(0), meta(0)),
        scratch_shapes=[
            pltpu.VMEM((hp, SEQ + 2 * NA_PAD, HEAD_DIM), BF16),
            pltpu.VMEM((hp, SEQ + 2 * NA_PAD, HEAD_DIM), BF16),
            pltpu.VMEM((hp, META_PAD, HEAD_DIM), BF16),
            pltpu.VMEM((hp, META_PAD, HEAD_DIM), BF16),
            pltpu.VMEM((2 * NA_ROWS - 1, GRID_W, LANES), F32),
            pltpu.VMEM((hp, 3, Q_TILE, NA_KEYS), F32),
        ],
        compiler_params=_compiler_params(3),
        name="na_attention",
    )(qkvz, qkvz, qkvz, qkvz, qkvz_m, qkvz_m, qkvz_m, qkvz_m, rpb_pad)


def _diff_kernel(slopes_ref, q_ref, k_ref, v_ref, z_ref, qm_ref, km_ref, vm_ref, zm_ref,
                 lq1_ref, lk1_ref, lq2_ref, lk2_ref, subg_ref, y_ref, ym_ref,
                 kaug, kmp, vmp, diag, sbuf, *, lambda_init):
    h = pl.program_id(0)
    b = pl.program_id(1)
    ns = pl.program_id(2)
    rows_meta = Q_TILE + N_META
    n_chunk = SEQ // DIFF_KC
    meta_mask = _meta_lane_mask()
    lam = (jnp.exp(jnp.sum(lq1_ref[...] * lk1_ref[...], -1, keepdims=True))
           - jnp.exp(jnp.sum(lq2_ref[...] * lk2_ref[...], -1, keepdims=True)) + lambda_init)
    rate = LOG2E * slopes_ref[h]
    c = jnp.full((1, LANES), rate, F32)
    c1 = c.astype(BF16).astype(F32)
    c2 = (c - c1).astype(BF16).astype(F32)
    c3 = (c - c1 - c2).astype(BF16).astype(F32)

    def rate_piece(lane, first):
        return jnp.where(lane < first + 2, c1, jnp.where(lane < first + 4, c2, c3))

    @pl.when(jnp.logical_and(b == 0, ns == 0))
    def _():
        kpos = lax.broadcasted_iota(jnp.int32, (SEQ, LANES), 0)
        lane = lax.broadcasted_iota(jnp.int32, (SEQ, LANES), 1)
        k_lo = (kpos & (DIFF_KC - 1)).astype(F32)
        k_hi = (kpos - (kpos & (DIFF_KC - 1))).astype(F32)
        pair = jnp.where((lane & 1) == 0, k_hi, k_lo)
        feat = jnp.where(lane < 6, -rate_piece(lane, 0), jnp.where(lane < 12, pair, 0.0))
        for j in range(2):
            kaug[j, :, HEAD_DIM:2 * HEAD_DIM] = feat.astype(BF16)
        qi = lax.broadcasted_iota(jnp.int32, (rows_meta, DIFF_KC), 0)
        kj = lax.broadcasted_iota(jnp.int32, (rows_meta, DIFF_KC), 1)
        diag[...] = jnp.where(qi < Q_TILE, -rate * jnp.abs(qi - kj).astype(F32), 0.0)

    @pl.when(ns == 0)
    def _():
        vmp[...] = jnp.zeros((META_PAD, 2 * HEAD_DIM), BF16)
        for j in range(2):
            kaug[j, :, 0:HEAD_DIM] = k_ref[j]
            kmp[j] = jnp.zeros((META_PAD, HEAD_DIM), BF16)
            kmp[j, 0:N_META] = km_ref[j]
            vmp[0:N_META, j * HEAD_DIM:(j + 1) * HEAD_DIM] = vm_ref[j]

    maps = range(2)

    def score_tile(ti, n, rows):
        row = lax.broadcasted_iota(jnp.int32, (rows, LANES), 0)
        lane = lax.broadcasted_iota(jnp.int32, (rows, LANES), 1)
        pair = jnp.where((lane & 1) == 0, (n * Q_TILE).astype(F32), row.astype(F32))
        qfeat = jnp.where(lane < 6, pair, jnp.where(lane < 12, rate_piece(lane, 6), 0.0))
        qfeat = jnp.where(row < Q_TILE, qfeat, 0.0)
        feat_before = qfeat.astype(BF16)
        feat_after = (-qfeat).astype(BF16)
        feat_diag = jnp.zeros((rows, LANES), BF16)
        q_rows = slice(ti * Q_TILE, (ti + 1) * Q_TILE)
        qs = [q_ref[j, q_rows, :] for j in maps]
        if rows > Q_TILE:
            qs = [jnp.concatenate([qs[j], qm_ref[j]], axis=0) for j in maps]
        mx = [None, None]
        for d in range(n_chunk):
            c = (n + d) & (n_chunk - 1)
            start = pl.multiple_of(c * DIFF_KC, DIFF_KC)
            f = feat_diag if d == 0 else jnp.where(c < n, feat_before, feat_after)
            for j in maps:
                s = _dot_nt(jnp.concatenate([qs[j], f], axis=1), kaug[j, pl.ds(start, DIFF_KC), :])
                if d == 0:
                    s = s + diag[0:rows, :]
                sbuf[ti, j, d, 0:rows, :] = s
                mx[j] = s if mx[j] is None else jnp.maximum(mx[j], s)
        s_m = [_dot_nt(qs[j], kmp[j]) + meta_mask for j in maps]
        m = [jnp.maximum(jnp.max(mx[j], -1, keepdims=True), jnp.max(s_m[j], -1, keepdims=True))
             for j in maps]
        return s_m, m

    def finish_tile(ti, n, rows, s_m, m):
        acc = [None, None]
        o = [None, None]
        for d in range(n_chunk):
            c = (n + d) & (n_chunk - 1)
            start = pl.multiple_of(c * DIFF_KC, DIFF_KC)
            v_chunk = jnp.concatenate(
                [v_ref[0, pl.ds(start, DIFF_KC), :], v_ref[1, pl.ds(start, DIFF_KC), :]], axis=1)
            for j in maps:
                p = jnp.exp2(sbuf[ti, j, d, 0:rows, :] - m[j])
                acc[j] = p if acc[j] is None else acc[j] + p
                part = _dot(p.astype(BF16), v_chunk)
                o[j] = part if o[j] is None else o[j] + part
        outs = []
        for j in maps:
            p_m = jnp.exp2(s_m[j] - m[j])
            l = jnp.sum(acc[j], -1, keepdims=True) + jnp.sum(p_m, -1, keepdims=True)
            outs.append((o[j] + _dot(p_m.astype(BF16), vmp[...])) / l)

        o = outs[0] - lam * outs[1]
        o = o * lax.rsqrt(jnp.mean(o * o, -1, keepdims=True) + RMS_EPS)
        o = o * subg_ref[...] * (1.0 - lambda_init)
        q_rows = slice(ti * Q_TILE, (ti + 1) * Q_TILE)
        z = jnp.concatenate([z_ref[0, q_rows, :], z_ref[1, q_rows, :]], axis=1)
        if rows > Q_TILE:
            z = jnp.concatenate([z, jnp.concatenate([zm_ref[0], zm_ref[1]], axis=1)], axis=0)
        res = (o * _silu(z.astype(F32))).astype(BF16)
        for j in maps:
            y_ref[j, q_rows, :] = res[:Q_TILE, j * HEAD_DIM:(j + 1) * HEAD_DIM]
            if rows > Q_TILE:
                ym_ref[j] = res[Q_TILE:, j * HEAD_DIM:(j + 1) * HEAD_DIM]

    tiles = [(ti, ns * TILES_PER_STEP + ti, rows_meta if ti == 0 else Q_TILE)
             for ti in range(TILES_PER_STEP)]
    stats = [score_tile(*tile) for tile in tiles]
    for tile, (s_m, m) in zip(tiles, stats):
        finish_tile(*tile, s_m, m)


def _diff_attention(qkvz, qkvz_m, lq1, lk1, lq2, lk2, subg, layer_idx, bn):
    n_blocks = SEQ // Q_TILE
    n_steps = n_blocks // TILES_PER_STEP
    assert SEQ // DIFF_KC == n_blocks and Q_TILE == DIFF_KC
    lambda_init = 0.8 - 0.6 * math.exp(-0.3 * layer_idx)

    def real(s):
        return pl.BlockSpec((2, Q_STEP, HEAD_DIM), lambda h, b, n: (s * B_HEADS + h, b * n_steps + n, 0))

    def full(s):
        return pl.BlockSpec((2, SEQ, HEAD_DIM), lambda h, b, n: (s * B_HEADS + h, b, 0))

    def meta(s):
        return pl.BlockSpec((2, N_META, HEAD_DIM), lambda h, b, n: (s * B_HEADS + h, b, 0))

    def vec(width):
        return pl.BlockSpec((1, width), lambda h, b, n: (0, 0))

    rows = Q_TILE + N_META
    return pl.pallas_call(
        functools.partial(_diff_kernel, lambda_init=lambda_init),
        out_shape=(jax.ShapeDtypeStruct((2 * B_HEADS, bn * SEQ, HEAD_DIM), BF16),
                   jax.ShapeDtypeStruct((2 * B_HEADS, bn * N_META, HEAD_DIM), BF16)),
        grid=(B_HEADS, bn, n_steps),
        in_specs=[
            pl.BlockSpec(memory_space=pltpu.SMEM),
            real(0), full(1), full(2), real(3), meta(0), meta(1), meta(2), meta(3),
            vec(HEAD_DIM), vec(HEAD_DIM), vec(HEAD_DIM), vec(HEAD_DIM), vec(2 * HEAD_DIM),
        ],
        out_specs=(real(0), meta(0)),
        scratch_shapes=[
            pltpu.VMEM((2, SEQ, 2 * HEAD_DIM), BF16),
            pltpu.VMEM((2, META_PAD, HEAD_DIM), BF16),
            pltpu.VMEM((META_PAD, 2 * HEAD_DIM), BF16),
            pltpu.VMEM((rows, DIFF_KC), F32),
            pltpu.VMEM((TILES_PER_STEP, 2, SEQ // DIFF_KC, rows, DIFF_KC), F32),
        ],
        compiler_params=_compiler_params(3),
        name="diff_attention",
    )(_alibi_slopes(B_HEADS), qkvz, qkvz, qkvz, qkvz, qkvz_m, qkvz_m, qkvz_m, qkvz_m,
      lq1.reshape(1, -1), lk1.reshape(1, -1), lq2.reshape(1, -1), lk2.reshape(1, -1),
      subg.reshape(1, -1))


def _swa_tables():
    qi = np.arange(Q_TILE)[:, None]
    kj = np.arange(SWA_WIN)[None, :]
    dist = np.abs(qi + C_WINDOW - kj)
    near = dist <= C_WINDOW
    valid = np.stack([near & (kj >= C_WINDOW), near, near & (kj < C_WINDOW + Q_TILE)])
    return dist.astype(np.float32), valid.astype(np.float32)


def _swa_kernel(slopes_ref, sink_ref, q_ref, k_ref, v_ref, z_ref, qm_ref, km_ref, vm_ref, zm_ref,
                dist_ref, valid_ref, y_ref, ym_ref, kpad, vpad, kmp, vmp, comb, *, n_blocks):
    kh = pl.program_id(0)
    b = pl.program_id(1)
    ns = pl.program_id(2)

    def stack_heads(x_ref):
        return jnp.concatenate([x_ref[gq] for gq in range(C_GROUP)], axis=0)

    def meta_sink_tile(gq, rows):
        lane = lax.broadcasted_iota(jnp.int32, (rows, META_PAD), 1)
        sink = LOG2E * sink_ref[kh * C_GROUP + gq]
        return jnp.where(lane < N_META, 0.0, jnp.where(lane == N_META, sink, NEG_INF))

    def gate_and_store(dst_ref, o, z_ref_, rows):
        for gq in range(C_GROUP):
            dst_ref[gq] = (o[gq * rows:(gq + 1) * rows] * _silu(z_ref_[gq].astype(F32))).astype(BF16)

    @pl.when(jnp.logical_and(b == 0, ns == 0))
    def _():
        for gq in range(C_GROUP):
            slope = LOG2E * slopes_ref[kh * C_GROUP + gq]
            rows = slice(gq * Q_TILE, (gq + 1) * Q_TILE)
            for c in range(3):
                comb[c, rows, 0:SWA_WIN] = jnp.where(valid_ref[c] > 0.5, -slope * dist_ref[...], NEG_INF)
                comb[c, rows, SWA_WIN:SWA_KEYS] = meta_sink_tile(gq, Q_TILE)

    @pl.when(ns == 0)
    def _():
        zeros = jnp.zeros((C_WINDOW, HEAD_DIM), BF16)
        for pad, src in ((kpad, k_ref), (vpad, v_ref)):
            pad[0:C_WINDOW] = zeros
            pad[C_WINDOW:C_WINDOW + SEQ] = src[...]
            pad[C_WINDOW + SEQ:SEQ + 2 * C_WINDOW] = zeros
        for pad, src in ((kmp, km_ref), (vmp, vm_ref)):
            pad[...] = jnp.zeros((META_PAD, HEAD_DIM), BF16)
            pad[0:N_META] = src[...]
        mask = jnp.concatenate([meta_sink_tile(gq, N_META) for gq in range(C_GROUP)], axis=0)
        s = _dot_nt(stack_heads(qm_ref), kmp[...]) + mask
        p = jnp.exp2(s - jnp.max(s, -1, keepdims=True))
        o = _dot(p.astype(BF16), vmp[...]) / jnp.sum(p, -1, keepdims=True)
        gate_and_store(ym_ref, o, zm_ref, N_META)

    tiles = []
    for ti in range(TILES_PER_STEP):
        n = ns * TILES_PER_STEP + ti
        start = pl.multiple_of(n * Q_TILE, Q_TILE)
        kind = jnp.where(n == 0, 0, jnp.where(n == n_blocks - 1, 2, 1))
        rows = slice(ti * Q_TILE, (ti + 1) * Q_TILE)
        keys = jnp.concatenate([kpad[pl.ds(start, SWA_WIN), :], kmp[...]], axis=0)
        q = jnp.concatenate([q_ref[gq, rows, :] for gq in range(C_GROUP)], axis=0)
        tiles.append((rows, start, _dot_nt(q, keys) + comb[kind]))
    for rows, start, s_all in tiles:
        vals = jnp.concatenate([vpad[pl.ds(start, SWA_WIN), :], vmp[...]], axis=0)
        for gq in range(C_GROUP):
            s = s_all[gq * Q_TILE:(gq + 1) * Q_TILE]
            p = jnp.exp2(s - jnp.max(s, -1, keepdims=True))
            o = _dot(p.astype(BF16), vals) / jnp.sum(p, -1, keepdims=True)
            y_ref[gq, rows, :] = (o * _silu(z_ref[gq, rows, :].astype(F32))).astype(BF16)


def _swa_attention(qkvz, qkvz_m, sink, bn):
    n_blocks = SEQ // Q_TILE
    n_steps = n_blocks // TILES_PER_STEP
    assert n_blocks >= 2
    k_cb0 = C_HEADS
    v_cb0 = k_cb0 + C_KV_HEADS
    z_grp0 = (v_cb0 + C_KV_HEADS) // C_GROUP
    dist, valid = _swa_tables()

    def q_like(grp0):
        return pl.BlockSpec((C_GROUP, Q_STEP, HEAD_DIM), lambda kh, b, n: (grp0 + kh, b * n_steps + n, 0))

    def q_like_meta(grp0):
        return pl.BlockSpec((C_GROUP, N_META, HEAD_DIM), lambda kh, b, n: (grp0 + kh, b, 0))

    def kv(cb0, rows):
        return pl.BlockSpec((None, rows, HEAD_DIM), lambda kh, b, n: (cb0 + kh, b, 0))

    smem = pl.BlockSpec(memory_space=pltpu.SMEM)
    return pl.pallas_call(
        functools.partial(_swa_kernel, n_blocks=n_blocks),
        out_shape=(jax.ShapeDtypeStruct((C_HEADS, bn * SEQ, HEAD_DIM), BF16),
                   jax.ShapeDtypeStruct((C_HEADS, bn * N_META, HEAD_DIM), BF16)),
        grid=(C_KV_HEADS, bn, n_steps),
        in_specs=[
            smem, smem,
            q_like(0), kv(k_cb0, SEQ), kv(v_cb0, SEQ), q_like(z_grp0),
            q_like_meta(0), kv(k_cb0, N_META), kv(v_cb0, N_META), q_like_meta(z_grp0),
            pl.BlockSpec((Q_TILE, SWA_WIN), lambda kh, b, n: (0, 0)),
            pl.BlockSpec((3, Q_TILE, SWA_WIN), lambda kh, b, n: (0, 0, 0)),
        ],
        out_specs=(q_like(0), q_like_meta(0)),
        scratch_shapes=[
            pltpu.VMEM((SEQ + 2 * C_WINDOW, HEAD_DIM), BF16),
            pltpu.VMEM((SEQ + 2 * C_WINDOW, HEAD_DIM), BF16),
            pltpu.VMEM((META_PAD, HEAD_DIM), BF16),
            pltpu.VMEM((META_PAD, HEAD_DIM), BF16),
            pltpu.VMEM((3, C_GROUP * Q_TILE, SWA_KEYS), F32),
        ],
        compiler_params=_compiler_params(3),
        name="swa_attention",
    )(_alibi_slopes(C_HEADS), sink.astype(F32), qkvz, qkvz, qkvz, qkvz,
      qkvz_m, qkvz_m, qkvz_m, qkvz_m, jnp.asarray(dist), jnp.asarray(valid))


def kernel(x, meta_tokens, w_in_a, rpb_a, w_in_b, lam_q1_b, lam_k1_b, lam_q2_b, lam_k2_b,
           subln_g_b, w_in_c, sink_c, w_out, ln_g, ln_b):
    bn, seq, d = x.shape
    assert seq == SEQ and d == D_MODEL
    h = x.reshape(bn * seq, d)
    hm = jnp.tile(meta_tokens.astype(F32), (bn, 1))
    h_bf, hm_bf = h, hm
    w_out_bf = w_out.astype(BF16)
    for i in range(DEPTH):
        kind, j = i % 3, i // 3
        if kind == 0:
            qkvz, qkvz_m = _inproj(h_bf, hm_bf, w_in_a, j, A_HEADS * HEAD_DIM)
            y, ym = _na_attention(qkvz, qkvz_m, rpb_a[j], bn)
        elif kind == 1:
            qkvz, qkvz_m = _inproj(h_bf, hm_bf, w_in_b, j, 2 * B_HEADS * HEAD_DIM)
            y, ym = _diff_attention(qkvz, qkvz_m, lam_q1_b[j], lam_k1_b[j], lam_q2_b[j],
                                    lam_k2_b[j], subln_g_b[j], i, bn)
        else:
            qkvz, qkvz_m = _inproj(h_bf, hm_bf, w_in_c, j, C_HEADS * HEAD_DIM)
            y, ym = _swa_attention(qkvz, qkvz_m, sink_c[j], bn)
        last = i == DEPTH - 1
        outs = _outproj_ln(y, ym, w_out_bf, i, h, hm, ln_g[i], ln_b[i], not last)
        if last:
            h, hm = outs
        else:
            h, hm, h_bf, hm_bf = outs
    return h.reshape(bn, seq, d)
```

```python
import functools
import math

import jax
import jax.numpy as jnp
import numpy as np
from jax import lax
from jax.experimental import pallas as pl
from jax.experimental.pallas import tpu as pltpu

F32 = jnp.float32
BF16 = jnp.bfloat16

D_MODEL = 2048
SEQ = 2048
DEPTH = 4
N_META = 16
GRID_W = 64
GRID_ROWS = SEQ // GRID_W
NA_ROWS = 8
NA_COLS = 16
HEAD_DIM = 128
A_HEADS = 16
B_HEADS = 8
C_HEADS = 16
C_KV_HEADS = 4
C_GROUP = C_HEADS // C_KV_HEADS
C_WINDOW = 128
ALPHA = (2 * DEPTH) ** 0.25
LN_EPS = 1e-5
RMS_EPS = 1e-5
NEG_INF = -1e30
LOG2E = math.log2(math.e)
Q_SCALE_LOG2 = LOG2E * HEAD_DIM ** -0.5

LANES = 128
META_PAD = LANES
VMEM_LIMIT = 56 * 1024 * 1024

PROJ_TM = 1024
PROJ_TN = 1024
LN_TM = 512
LN_SUB = 128
Q_TILE = 256
TILES_PER_STEP = 4
Q_STEP = TILES_PER_STEP * Q_TILE
NA_HEADS_PER_STEP = 4
NA_GROUP_ROWS = Q_TILE // GRID_W
NA_WIN_ROWS = NA_GROUP_ROWS + NA_ROWS
NA_WIN = NA_WIN_ROWS * GRID_W
NA_KEYS = NA_WIN + META_PAD
NA_PAD = (NA_ROWS // 2) * GRID_W
SWA_WIN = Q_TILE + 2 * C_WINDOW
SWA_KEYS = SWA_WIN + META_PAD
DIFF_KC = 256


def _dot_nt(a, b):
    return lax.dot_general(a, b, (((1,), (1,)), ((), ())), preferred_element_type=F32)


def _dot(a, b):
    return jnp.dot(a, b, preferred_element_type=F32)


def _silu(z):
    return z / (1.0 + jnp.exp(-z))


def _meta_lane_mask():
    lane = lax.broadcasted_iota(jnp.int32, (1, META_PAD), 1)
    return jnp.where(lane < N_META, 0.0, NEG_INF).astype(F32)


def _compiler_params(n_axes):
    return pltpu.CompilerParams(
        dimension_semantics=("arbitrary",) * n_axes, vmem_limit_bytes=VMEM_LIMIT)


def _alibi_slopes(n_heads):
    return jnp.asarray(np.exp2(-8.0 * np.arange(1, n_heads + 1) / n_heads), F32)


def _inproj_kernel(x_ref, xm_ref, w_ref, o_ref, om_ref, wbf_ref, *, n_q_tiles):
    j = pl.program_id(0)
    i = pl.program_id(1)
    scale = jnp.where(j < n_q_tiles, Q_SCALE_LOG2, 1.0).astype(F32)

    def put(dst, acc):
        for c in range(PROJ_TN // LANES):
            dst[c] = (acc[:, c * LANES:(c + 1) * LANES] * scale).astype(BF16)

    @pl.when(i == 0)
    def _():
        wbf_ref[...] = w_ref[...].astype(BF16)
        put(om_ref, _dot(xm_ref[...].astype(BF16), wbf_ref[...]))

    put(o_ref, _dot(x_ref[...].astype(BF16), wbf_ref[...]))


def _inproj(x, xm, w_stack, layer, q_cols):
    m, d = x.shape
    mm = xm.shape[0]
    n = w_stack.shape[2]
    tm = min(PROJ_TM, m)
    cb = PROJ_TN // LANES
    assert m % tm == 0 and n % PROJ_TN == 0 and q_cols % PROJ_TN == 0
    kern = functools.partial(_inproj_kernel, n_q_tiles=q_cols // PROJ_TN)
    return pl.pallas_call(
        kern,
        out_shape=(jax.ShapeDtypeStruct((n // LANES, m, LANES), BF16),
                   jax.ShapeDtypeStruct((n // LANES, mm, LANES), BF16)),
        grid=(n // PROJ_TN, m // tm),
        in_specs=[
            pl.BlockSpec((tm, d), lambda j, i: (i, 0)),
            pl.BlockSpec((mm, d), lambda j, i: (0, 0)),
            pl.BlockSpec((None, d, PROJ_TN), lambda j, i: (layer, 0, j)),
        ],
        out_specs=(
            pl.BlockSpec((cb, tm, LANES), lambda j, i: (j, i, 0)),
            pl.BlockSpec((cb, mm, LANES), lambda j, i: (j, 0, 0)),
        ),
        scratch_shapes=[pltpu.VMEM((d, PROJ_TN), BF16)],
        compiler_params=_compiler_params(2),
        name="inproj",
    )(x, xm, w_stack)


def _outproj_ln_kernel(y_ref, ym_ref, w_ref, h_ref, hm_ref, g_ref, b_ref, *out_refs, emit_bf16):
    def layer_norm(y_cb, rows, h):
        y = jnp.concatenate([y_cb[c, rows, :] for c in range(y_cb.shape[0])], axis=1)
        t = ALPHA * h + _dot(y, w_ref[...])
        mu = jnp.mean(t, -1, keepdims=True)
        d = t - mu
        var = jnp.mean(d * d, -1, keepdims=True)
        return d * lax.rsqrt(var + LN_EPS) * g_ref[...] + b_ref[...]

    if emit_bf16:
        o_ref, om_ref, obf_ref, ombf_ref = out_refs
    else:
        o_ref, om_ref = out_refs

    @pl.when(pl.program_id(0) == 0)
    def _():
        r = layer_norm(ym_ref, slice(None), hm_ref[...])
        om_ref[...] = r
        if emit_bf16:
            ombf_ref[...] = r.astype(BF16)

    for t in range(LN_TM // LN_SUB):
        rows = slice(t * LN_SUB, (t + 1) * LN_SUB)
        r = layer_norm(y_ref, rows, h_ref[rows, :])
        o_ref[rows, :] = r
        if emit_bf16:
            obf_ref[rows, :] = r.astype(BF16)


def _outproj_ln(y, ym, w_stack_bf, layer, h, hm, g, b, emit_bf16):
    m, d = h.shape
    mm = hm.shape[0]
    cb = d // LANES
    assert m % LN_TM == 0
    row = pl.BlockSpec((LN_TM, d), lambda i: (i, 0))
    meta = pl.BlockSpec((mm, d), lambda i: (0, 0))
    vec = pl.BlockSpec((1, d), lambda i: (0, 0))
    out_shape = [jax.ShapeDtypeStruct((m, d), F32), jax.ShapeDtypeStruct((mm, d), F32)]
    out_specs = [row, meta]
    if emit_bf16:
        out_shape += [jax.ShapeDtypeStruct((m, d), BF16), jax.ShapeDtypeStruct((mm, d), BF16)]
        out_specs += [row, meta]
    return pl.pallas_call(
        functools.partial(_outproj_ln_kernel, emit_bf16=emit_bf16),
        out_shape=tuple(out_shape),
        grid=(m // LN_TM,),
        in_specs=[
            pl.BlockSpec((cb, LN_TM, LANES), lambda i: (0, i, 0)),
            pl.BlockSpec((cb, mm, LANES), lambda i: (0, 0, 0)),
            pl.BlockSpec((None, d, d), lambda i: (layer, 0, 0), pipeline_mode=pl.Buffered(1)),
            row, meta, vec, vec,
        ],
        out_specs=tuple(out_specs),
        compiler_params=_compiler_params(1),
        name="outproj_ln",
    )(y, ym, w_stack_bf, h, hm, g.reshape(1, d), b.reshape(1, d))


def _na_tables():
    qa = np.arange(NA_GROUP_ROWS)
    ke = np.arange(NA_WIN_ROWS)
    dr_idx = ke[None, :] - NA_ROWS // 2 - qa[:, None] + NA_ROWS - 1
    n_groups = GRID_ROWS // NA_GROUP_ROWS
    oks = []
    for g in range(n_groups):
        r = NA_GROUP_ROWS * g + qa
        r0 = np.clip(r - NA_ROWS // 2, 0, GRID_ROWS - NA_ROWS)
        kr = NA_GROUP_ROWS * g - NA_ROWS // 2 + ke
        oks.append((kr[None, :] >= r0[:, None]) & (kr[None, :] < r0[:, None] + NA_ROWS))
    for g in range(1, n_groups - 1):
        assert np.array_equal(oks[g], oks[1])
    row_ok = np.stack([oks[0], oks[1], oks[n_groups - 1]])
    assert dr_idx[row_ok.any(0)].min() >= 0 and dr_idx[row_ok.any(0)].max() <= 2 * NA_ROWS - 2
    return dr_idx, row_ok


_NA_TABLES = _na_tables()


def _na_build_bias(rpb_ref, toep, comb):
    dr_idx, row_ok = _NA_TABLES
    qc = lax.broadcasted_iota(jnp.int32, (GRID_W, LANES), 0)
    lane = lax.broadcasted_iota(jnp.int32, (GRID_W, LANES), 1)
    kc = lane & (GRID_W - 1)
    c0 = jnp.clip(qc - NA_COLS // 2, 0, GRID_W - NA_COLS)
    col_ok = jnp.logical_and(kc >= c0, kc < c0 + NA_COLS)
    dc_idx = jnp.clip(kc - qc + NA_COLS - 1, 0, LANES - 1)
    for dr in range(2 * NA_ROWS - 1):
        row = jnp.broadcast_to(rpb_ref[dr:dr + 1, :], (GRID_W, LANES))
        toep[dr] = jnp.where(col_ok, LOG2E * jnp.take_along_axis(row, dc_idx, axis=1), NEG_INF)
    masked = jnp.full((GRID_W, LANES), NEG_INF, F32)
    left_half = lane < GRID_W
    meta_tile = jnp.broadcast_to(_meta_lane_mask(), (Q_TILE, META_PAD))
    for kind in range(3):
        for a in range(NA_GROUP_ROWS):
            for ep in range(NA_WIN_ROWS // 2):
                halves = [toep[int(dr_idx[a, e])] if row_ok[kind, a, e] else masked
                          for e in (2 * ep, 2 * ep + 1)]
                comb[kind, a * GRID_W:(a + 1) * GRID_W, ep * LANES:(ep + 1) * LANES] = jnp.where(
                    left_half, halves[0], halves[1])
        comb[kind, :, NA_WIN:NA_KEYS] = meta_tile


def _na_kernel(q_ref, k_ref, v_ref, z_ref, qm_ref, km_ref, vm_ref, zm_ref, rpb_ref,
               y_ref, ym_ref, kpad, vpad, kmp, vmp, toep, comb, *, n_groups):
    b = pl.program_id(1)
    gs = pl.program_id(2)
    heads = q_ref.shape[0]

    @pl.when(jnp.logical_and(b == 0, gs == 0))
    def _():
        for hh in range(heads):
            _na_build_bias(rpb_ref.at[hh], toep, comb.at[hh])

    @pl.when(gs == 0)
    def _():
        zeros = jnp.zeros((NA_PAD, HEAD_DIM), BF16)
        for hh in range(heads):
            for pad, src in ((kpad, k_ref), (vpad, v_ref)):
                pad[hh, 0:NA_PAD] = zeros
                pad[hh, NA_PAD:NA_PAD + SEQ] = src[hh]
                pad[hh, NA_PAD + SEQ:NA_PAD + SEQ + NA_PAD] = zeros
            for pad, src in ((kmp, km_ref), (vmp, vm_ref)):
                pad[hh] = jnp.zeros((META_PAD, HEAD_DIM), BF16)
                pad[hh, 0:N_META] = src[hh]
            s = _dot_nt(qm_ref[hh], kmp[hh]) + _meta_lane_mask()
            p = jnp.exp2(s - jnp.max(s, -1, keepdims=True))
            o = _dot(p.astype(BF16), vmp[hh]) / jnp.sum(p, -1, keepdims=True)
            ym_ref[hh] = (o * _silu(zm_ref[hh].astype(F32))).astype(BF16)

    chains = []
    for ti in range(TILES_PER_STEP):
        g = gs * TILES_PER_STEP + ti
        start = pl.multiple_of(g * Q_TILE, Q_TILE)
        kind = jnp.where(g == 0, 0, jnp.where(g == n_groups - 1, 2, 1))
        rows = slice(ti * Q_TILE, (ti + 1) * Q_TILE)
        for hh in range(heads):
            keys = jnp.concatenate([kpad[hh, pl.ds(start, NA_WIN), :], kmp[hh]], axis=0)
            chains.append((hh, rows, start, _dot_nt(q_ref[hh, rows, :], keys) + comb[hh, kind]))
    for hh, rows, start, s in chains:
        p = jnp.exp2(s - jnp.max(s, -1, keepdims=True))
        vals = jnp.concatenate([vpad[hh, pl.ds(start, NA_WIN), :], vmp[hh]], axis=0)
        o = _dot(p.astype(BF16), vals) / jnp.sum(p, -1, keepdims=True)
        y_ref[hh, rows, :] = (o * _silu(z_ref[hh, rows, :].astype(F32))).astype(BF16)


def _na_attention(qkvz, qkvz_m, rpb, bn):
    hp = NA_HEADS_PER_STEP
    n_groups = SEQ // Q_TILE
    n_steps = n_groups // TILES_PER_STEP
    n_dr, n_dc = rpb.shape[1:]
    rpb_pad = jnp.pad(rpb.astype(F32), ((0, 0), (0, 16 - n_dr), (0, LANES - n_dc)))
    sec = A_HEADS // hp

    def real(s):
        return pl.BlockSpec((hp, Q_STEP, HEAD_DIM), lambda h, b, g: (s * sec + h, b * n_steps + g, 0))

    def full(s):
        return pl.BlockSpec((hp, SEQ, HEAD_DIM), lambda h, b, g: (s * sec + h, b, 0))

    def meta(s):
        return pl.BlockSpec((hp, N_META, HEAD_DIM), lambda h, b, g: (s * sec + h, b, 0))

    return pl.pallas_call(
        functools.partial(_na_kernel, n_groups=n_groups),
        out_shape=(jax.ShapeDtypeStruct((A_HEADS, bn * SEQ, HEAD_DIM), BF16),
                   jax.ShapeDtypeStruct((A_HEADS, bn * N_META, HEAD_DIM), BF16)),
        grid=(sec, bn, n_steps),
        in_specs=[
            real(0), full(1), full(2), real(3), meta(0), meta(1), meta(2), meta(3),
            pl.BlockSpec((hp, 16, LANES), lambda h, b, g: (h, 0, 0)),
        ],
        out_specs=(real(0), meta(0)),
        scratch_shapes=[
            pltpu.VMEM((hp, SEQ + 2 * NA_PAD, HEAD_DIM), BF16),
            pltpu.VMEM((hp, SEQ + 2 * NA_PAD, HEAD_DIM), BF16),
            pltpu.VMEM((hp, META_PAD, HEAD_DIM), BF16),
            pltpu.VMEM((hp, META_PAD, HEAD_DIM), BF16),
            pltpu.VMEM((2 * NA_ROWS - 1, GRID_W, LANES), F32),
            pltpu.VMEM((hp, 3, Q_TILE, NA_KEYS), F32),
        ],
        compiler_params=_compiler_params(3),
        name="na_attention",
    )(qkvz, qkvz, qkvz, qkvz, qkvz_m, qkvz_m, qkvz_m, qkvz_m, rpb_pad)


def _diff_kernel(slopes_ref, q_ref, k_ref, v_ref, z_ref, qm_ref, km_ref, vm_ref, zm_ref,
                 lq1_ref, lk1_ref, lq2_ref, lk2_ref, subg_ref, y_ref, ym_ref,
                 kaug, kmp, vmp, diag, sbuf, *, lambda_init):
    h = pl.program_id(0)
    b = pl.program_id(1)
    ns = pl.program_id(2)
    rows_meta = Q_TILE + N_META
    n_chunk = SEQ // DIFF_KC
    meta_mask = _meta_lane_mask()
    lam = (jnp.exp(jnp.sum(lq1_ref[...] * lk1_ref[...], -1, keepdims=True))
           - jnp.exp(jnp.sum(lq2_ref[...] * lk2_ref[...], -1, keepdims=True)) + lambda_init)
    rate = LOG2E * slopes_ref[h]
    c = jnp.full((1, LANES), rate, F32)
    c1 = c.astype(BF16).astype(F32)
    c2 = (c - c1).astype(BF16).astype(F32)
    c3 = (c - c1 - c2).astype(BF16).astype(F32)

    def rate_piece(lane, first):
        return jnp.where(lane < first + 2, c1, jnp.where(lane < first + 4, c2, c3))

    @pl.when(jnp.logical_and(b == 0, ns == 0))
    def _():
        kpos = lax.broadcasted_iota(jnp.int32, (SEQ, LANES), 0)
        lane = lax.broadcasted_iota(jnp.int32, (SEQ, LANES), 1)
        k_lo = (kpos & (DIFF_KC - 1)).astype(F32)
        k_hi = (kpos - (kpos & (DIFF_KC - 1))).astype(F32)
        pair = jnp.where((lane & 1) == 0, k_hi, k_lo)
        feat = jnp.where(lane < 6, -rate_piece(lane, 0), jnp.where(lane < 12, pair, 0.0))
        for j in range(2):
            kaug[j, :, HEAD_DIM:2 * HEAD_DIM] = feat.astype(BF16)
        qi = lax.broadcasted_iota(jnp.int32, (rows_meta, DIFF_KC), 0)
        kj = lax.broadcasted_iota(jnp.int32, (rows_meta, DIFF_KC), 1)
        diag[...] = jnp.where(qi < Q_TILE, -rate * jnp.abs(qi - kj).astype(F32), 0.0)

    @pl.when(ns == 0)
    def _():
        vmp[...] = jnp.zeros((META_PAD, 2 * HEAD_DIM), BF16)
        for j in range(2):
            kaug[j, :, 0:HEAD_DIM] = k_ref[j]
            kmp[j] = jnp.zeros((META_PAD, HEAD_DIM), BF16)
            kmp[j, 0:N_META] = km_ref[j]
            vmp[0:N_META, j * HEAD_DIM:(j + 1) * HEAD_DIM] = vm_ref[j]

    maps = range(2)

    def score_tile(ti, n, rows):
        row = lax.broadcasted_iota(jnp.int32, (rows, LANES), 0)
        lane = lax.broadcasted_iota(jnp.int32, (rows, LANES), 1)
        pair = jnp.where((lane & 1) == 0, (n * Q_TILE).astype(F32), row.astype(F32))
        qfeat = jnp.where(lane < 6, pair, jnp.where(lane < 12, rate_piece(lane, 6), 0.0))
        qfeat = jnp.where(row < Q_TILE, qfeat, 0.0)
        feat_before = qfeat.astype(BF16)
        feat_after = (-qfeat).astype(BF16)
        feat_diag = jnp.zeros((rows, LANES), BF16)
        q_rows = slice(ti * Q_TILE, (ti + 1) * Q_TILE)
        qs = [q_ref[j, q_rows, :] for j in maps]
        if rows > Q_TILE:
            qs = [jnp.concatenate([qs[j], qm_ref[j]], axis=0) for j in maps]
        mx = [None, None]
        for d in range(n_chunk):
            c = (n + d) & (n_chunk - 1)
            start = pl.multiple_of(c * DIFF_KC, DIFF_KC)
            f = feat_diag if d == 0 else jnp.where(c < n, feat_before, feat_after)
            for j in maps:
                s = _dot_nt(jnp.concatenate([qs[j], f], axis=1), kaug[j, pl.ds(start, DIFF_KC), :])
                if d == 0:
                    s = s + diag[0:rows, :]
                sbuf[ti, j, d, 0:rows, :] = s
                mx[j] = s if mx[j] is None else jnp.maximum(mx[j], s)
        s_m = [_dot_nt(qs[j], kmp[j]) + meta_mask for j in maps]
        m = [jnp.maximum(jnp.max(mx[j], -1, keepdims=True), jnp.max(s_m[j], -1, keepdims=True))
             for j in maps]
        return s_m, m

    def finish_tile(ti, n, rows, s_m, m):
        acc = [None, None]
        o = [None, None]
        for d in range(n_chunk):
            c = (n + d) & (n_chunk - 1)
            start = pl.multiple_of(c * DIFF_KC, DIFF_KC)
            v_chunk = jnp.concatenate(
                [v_ref[0, pl.ds(start, DIFF_KC), :], v_ref[1, pl.ds(start, DIFF_KC), :]], axis=1)
            for j in maps:
                p = jnp.exp2(sbuf[ti, j, d, 0:rows, :] - m[j])
                acc[j] = p if acc[j] is None else acc[j] + p
                part = _dot(p.astype(BF16), v_chunk)
                o[j] = part if o[j] is None else o[j] + part
        outs = []
        for j in maps:
            p_m = jnp.exp2(s_m[j] - m[j])
            l = jnp.sum(acc[j], -1, keepdims=True) + jnp.sum(p_m, -1, keepdims=True)
            outs.append((o[j] + _dot(p_m.astype(BF16), vmp[...])) / l)

        o = outs[0] - lam * outs[1]
        o = o * lax.rsqrt(jnp.mean(o * o, -1, keepdims=True) + RMS_EPS)
        o = o * subg_ref[...] * (1.0 - lambda_init)
        q_rows = slice(ti * Q_TILE, (ti + 1) * Q_TILE)
        z = jnp.concatenate([z_ref[0, q_rows, :], z_ref[1, q_rows, :]], axis=1)
        if rows > Q_TILE:
            z = jnp.concatenate([z, jnp.concatenate([zm_ref[0], zm_ref[1]], axis=1)], axis=0)
        res = (o * _silu(z.astype(F32))).astype(BF16)
        for j in maps:
            y_ref[j, q_rows, :] = res[:Q_TILE, j * HEAD_DIM:(j + 1) * HEAD_DIM]
            if rows > Q_TILE:
                ym_ref[j] = res[Q_TILE:, j * HEAD_DIM:(j + 1) * HEAD_DIM]

    tiles = [(ti, ns * TILES_PER_STEP + ti, rows_meta if ti == 0 else Q_TILE)
             for ti in range(TILES_PER_STEP)]
    stats = [score_tile(*tile) for tile in tiles]
    for tile, (s_m, m) in zip(tiles, stats):
        finish_tile(*tile, s_m, m)


def _diff_attention(qkvz, qkvz_m, lq1, lk1, lq2, lk2, subg, layer_idx, bn):
    n_blocks = SEQ // Q_TILE
    n_steps = n_blocks // TILES_PER_STEP
    assert SEQ // DIFF_KC == n_blocks and Q_TILE == DIFF_KC
    lambda_init = 0.8 - 0.6 * math.exp(-0.3 * layer_idx)

    def real(s):
        return pl.BlockSpec((2, Q_STEP, HEAD_DIM), lambda h, b, n: (s * B_HEADS + h, b * n_steps + n, 0))

    def full(s):
        return pl.BlockSpec((2, SEQ, HEAD_DIM), lambda h, b, n: (s * B_HEADS + h, b, 0))

    def meta(s):
        return pl.BlockSpec((2, N_META, HEAD_DIM), lambda h, b, n: (s * B_HEADS + h, b, 0))

    def vec(width):
        return pl.BlockSpec((1, width), lambda h, b, n: (0, 0))

    rows = Q_TILE + N_META
    return pl.pallas_call(
        functools.partial(_diff_kernel, lambda_init=lambda_init),
        out_shape=(jax.ShapeDtypeStruct((2 * B_HEADS, bn * SEQ, HEAD_DIM), BF16),
                   jax.ShapeDtypeStruct((2 * B_HEADS, bn * N_META, HEAD_DIM), BF16)),
        grid=(B_HEADS, bn, n_steps),
        in_specs=[
            pl.BlockSpec(memory_space=pltpu.SMEM),
            real(0), full(1), full(2), real(3), meta(0), meta(1), meta(2), meta(3),
            vec(HEAD_DIM), vec(HEAD_DIM), vec(HEAD_DIM), vec(HEAD_DIM), vec(2 * HEAD_DIM),
        ],
        out_specs=(real(0), meta(0)),
        scratch_shapes=[
            pltpu.VMEM((2, SEQ, 2 * HEAD_DIM), BF16),
            pltpu.VMEM((2, META_PAD, HEAD_DIM), BF16),
            pltpu.VMEM((META_PAD, 2 * HEAD_DIM), BF16),
            pltpu.VMEM((rows, DIFF_KC), F32),
            pltpu.VMEM((TILES_PER_STEP, 2, SEQ // DIFF_KC, rows, DIFF_KC), F32),
        ],
        compiler_params=_compiler_params(3),
        name="diff_attention",
    )(_alibi_slopes(B_HEADS), qkvz, qkvz, qkvz, qkvz, qkvz_m, qkvz_m, qkvz_m, qkvz_m,
      lq1.reshape(1, -1), lk1.reshape(1, -1), lq2.reshape(1, -1), lk2.reshape(1, -1),
      subg.reshape(1, -1))


def _swa_tables():
    qi = np.arange(Q_TILE)[:, None]
    kj = np.arange(SWA_WIN)[None, :]
    dist = np.abs(qi + C_WINDOW - kj)
    near = dist <= C_WINDOW
    valid = np.stack([near & (kj >= C_WINDOW), near, near & (kj < C_WINDOW + Q_TILE)])
    return dist.astype(np.float32), valid.astype(np.float32)


def _swa_kernel(slopes_ref, sink_ref, q_ref, k_ref, v_ref, z_ref, qm_ref, km_ref, vm_ref, zm_ref,
                dist_ref, valid_ref, y_ref, ym_ref, kpad, vpad, kmp, vmp, comb, *, n_blocks):
    kh = pl.program_id(0)
    b = pl.program_id(1)
    ns = pl.program_id(2)

    def stack_heads(x_ref):
        return jnp.concatenate([x_ref[gq] for gq in range(C_GROUP)], axis=0)

    def meta_sink_tile(gq, rows):
        lane = lax.broadcasted_iota(jnp.int32, (rows, META_PAD), 1)
        sink = LOG2E * sink_ref[kh * C_GROUP + gq]
        return jnp.where(lane < N_META, 0.0, jnp.where(lane == N_META, sink, NEG_INF))

    def gate_and_store(dst_ref, o, z_ref_, rows):
        for gq in range(C_GROUP):
            dst_ref[gq] = (o[gq * rows:(gq + 1) * rows] * _silu(z_ref_[gq].astype(F32))).astype(BF16)

    @pl.when(jnp.logical_and(b == 0, ns == 0))
    def _():
        for gq in range(C_GROUP):
            slope = LOG2E * slopes_ref[kh * C_GROUP + gq]
            rows = slice(gq * Q_TILE, (gq + 1) * Q_TILE)
            for c in range(3):
                comb[c, rows, 0:SWA_WIN] = jnp.where(valid_ref[c] > 0.5, -slope * dist_ref[...], NEG_INF)
                comb[c, rows, SWA_WIN:SWA_KEYS] = meta_sink_tile(gq, Q_TILE)

    @pl.when(ns == 0)
    def _():
        zeros = jnp.zeros((C_WINDOW, HEAD_DIM), BF16)
        for pad, src in ((kpad, k_ref), (vpad, v_ref)):
            pad[0:C_WINDOW] = zeros
            pad[C_WINDOW:C_WINDOW + SEQ] = src[...]
            pad[C_WINDOW + SEQ:SEQ + 2 * C_WINDOW] = zeros
        for pad, src in ((kmp, km_ref), (vmp, vm_ref)):
            pad[...] = jnp.zeros((META_PAD, HEAD_DIM), BF16)
            pad[0:N_META] = src[...]
        mask = jnp.concatenate([meta_sink_tile(gq, N_META) for gq in range(C_GROUP)], axis=0)
        s = _dot_nt(stack_heads(qm_ref), kmp[...]) + mask
        p = jnp.exp2(s - jnp.max(s, -1, keepdims=True))
        o = _dot(p.astype(BF16), vmp[...]) / jnp.sum(p, -1, keepdims=True)
        gate_and_store(ym_ref, o, zm_ref, N_META)

    tiles = []
    for ti in range(TILES_PER_STEP):
        n = ns * TILES_PER_STEP + ti
        start = pl.multiple_of(n * Q_TILE, Q_TILE)
        kind = jnp.where(n == 0, 0, jnp.where(n == n_blocks - 1, 2, 1))
        rows = slice(ti * Q_TILE, (ti + 1) * Q_TILE)
        keys = jnp.concatenate([kpad[pl.ds(start, SWA_WIN), :], kmp[...]], axis=0)
        q = jnp.concatenate([q_ref[gq, rows, :] for gq in range(C_GROUP)], axis=0)
        tiles.append((rows, start, _dot_nt(q, keys) + comb[kind]))
    for rows, start, s_all in tiles:
        vals = jnp.concatenate([vpad[pl.ds(start, SWA_WIN), :], vmp[...]], axis=0)
        for gq in range(C_GROUP):
            s = s_all[gq * Q_TILE:(gq + 1) * Q_TILE]
            p = jnp.exp2(s - jnp.max(s, -1, keepdims=True))
            o = _dot(p.astype(BF16), vals) / jnp.sum(p, -1, keepdims=True)
            y_ref[gq, rows, :] = (o * _silu(z_ref[gq, rows, :].astype(F32))).astype(BF16)


def _swa_attention(qkvz, qkvz_m, sink, bn):
    n_blocks = SEQ // Q_TILE
    n_steps = n_blocks // TILES_PER_STEP
    assert n_blocks >= 2
    k_cb0 = C_HEADS
    v_cb0 = k_cb0 + C_KV_HEADS
    z_grp0 = (v_cb0 + C_KV_HEADS) // C_GROUP
    dist, valid = _swa_tables()

    def q_like(grp0):
        return pl.BlockSpec((C_GROUP, Q_STEP, HEAD_DIM), lambda kh, b, n: (grp0 + kh, b * n_steps + n, 0))

    def q_like_meta(grp0):
        return pl.BlockSpec((C_GROUP, N_META, HEAD_DIM), lambda kh, b, n: (grp0 + kh, b, 0))

    def kv(cb0, rows):
        return pl.BlockSpec((None, rows, HEAD_DIM), lambda kh, b, n: (cb0 + kh, b, 0))

    smem = pl.BlockSpec(memory_space=pltpu.SMEM)
    return pl.pallas_call(
        functools.partial(_swa_kernel, n_blocks=n_blocks),
        out_shape=(jax.ShapeDtypeStruct((C_HEADS, bn * SEQ, HEAD_DIM), BF16),
                   jax.ShapeDtypeStruct((C_HEADS, bn * N_META, HEAD_DIM), BF16)),
        grid=(C_KV_HEADS, bn, n_steps),
        in_specs=[
            smem, smem,
            q_like(0), kv(k_cb0, SEQ), kv(v_cb0, SEQ), q_like(z_grp0),
            q_like_meta(0), kv(k_cb0, N_META), kv(v_cb0, N_META), q_like_meta(z_grp0),
            pl.BlockSpec((Q_TILE, SWA_WIN), lambda kh, b, n: (0, 0)),
            pl.BlockSpec((3, Q_TILE, SWA_WIN), lambda kh, b, n: (0, 0, 0)),
        ],
        out_specs=(q_like(0), q_like_meta(0)),
        scratch_shapes=[
            pltpu.VMEM((SEQ + 2 * C_WINDOW, HEAD_DIM), BF16),
            pltpu.VMEM((SEQ + 2 * C_WINDOW, HEAD_DIM), BF16),
            pltpu.VMEM((META_PAD, HEAD_DIM), BF16),
            pltpu.VMEM((META_PAD, HEAD_DIM), BF16),
            pltpu.VMEM((3, C_GROUP * Q_TILE, SWA_KEYS), F32),
        ],
        compiler_params=_compiler_params(3),
        name="swa_attention",
    )(_alibi_slopes(C_HEADS), sink.astype(F32), qkvz, qkvz, qkvz, qkvz,
      qkvz_m, qkvz_m, qkvz_m, qkvz_m, jnp.asarray(dist), jnp.asarray(valid))


def kernel(x, meta_tokens, w_in_a, rpb_a, w_in_b, lam_q1_b, lam_k1_b, lam_q2_b, lam_k2_b,
           subln_g_b, w_in_c, sink_c, w_out, ln_g, ln_b):
    bn, seq, d = x.shape
    assert seq == SEQ and d == D_MODEL
    h = x.reshape(bn * seq, d)
    hm = jnp.tile(meta_tokens.astype(F32), (bn, 1))
    h_bf, hm_bf = h, hm
    w_out_bf = w_out.astype(BF16)
    for i in range(DEPTH):
        kind, j = i % 3, i // 3
        if kind == 0:
            qkvz, qkvz_m = _inproj(h_bf, hm_bf, w_in_a, j, A_HEADS * HEAD_DIM)
            y, ym = _na_attention(qkvz, qkvz_m, rpb_a[j], bn)
        elif kind == 1:
            qkvz, qkvz_m = _inproj(h_bf, hm_bf, w_in_b, j, 2 * B_HEADS * HEAD_DIM)
            y, ym = _diff_attention(qkvz, qkvz_m, lam_q1_b[j], lam_k1_b[j], lam_q2_b[j],
                                    lam_k2_b[j], subln_g_b[j], i, bn)
        else:
            qkvz, qkvz_m = _inproj(h_bf, hm_bf, w_in_c, j, C_HEADS * HEAD_DIM)
            y, ym = _swa_attention(qkvz, qkvz_m, sink_c[j], bn)
        last = i == DEPTH - 1
        outs = _outproj_ln(y, ym, w_out_bf, i, h, hm, ln_g[i], ln_b[i], not last)
        if last:
            h, hm = outs
        else:
            h, hm, h_bf, hm_bf = outs
    return h.reshape(bn, seq, d)
```

```python
import functools
import math

import jax
import jax.numpy as jnp
import numpy as np
from jax import lax
from jax.experimental import pallas as pl
from jax.experimental.pallas import tpu as pltpu

F32 = jnp.float32
BF16 = jnp.bfloat16

D_MODEL = 2048
SEQ = 2048
DEPTH = 4
N_META = 16
GRID_W = 64
GRID_ROWS = SEQ // GRID_W
NA_ROWS = 8
NA_COLS = 16
HEAD_DIM = 128
A_HEADS = 16
B_HEADS = 8
C_HEADS = 16
C_KV_HEADS = 4
C_GROUP = C_HEADS // C_KV_HEADS
C_WINDOW = 128
ALPHA = (2 * DEPTH) ** 0.25
LN_EPS = 1e-5
RMS_EPS = 1e-5
NEG_INF = -1e30
LOG2E = math.log2(math.e)
Q_SCALE_LOG2 = LOG2E * HEAD_DIM ** -0.5

LANES = 128
META_PAD = LANES
VMEM_LIMIT = 60 * 1024 * 1024

PROJ_X_TILE_BYTES = 8 * 1024 * 1024
PROJ_TN = 1024
LN_TM = 512
LN_SUB = 128
Q_TILE = 256
TILES_PER_STEP = 4
Q_STEP = TILES_PER_STEP * Q_TILE
NA_HEADS_PER_STEP = 4
NA_GROUP_ROWS = Q_TILE // GRID_W
NA_WIN_ROWS = NA_GROUP_ROWS + NA_ROWS
NA_WIN = NA_WIN_ROWS * GRID_W
NA_KEYS = NA_WIN + META_PAD
NA_PAD = (NA_ROWS // 2) * GRID_W
SWA_WIN = Q_TILE + 2 * C_WINDOW
SWA_KEYS = SWA_WIN + META_PAD
DIFF_KC = 256


def _dot_nt(a, b):
    return lax.dot_general(a, b, (((1,), (1,)), ((), ())), preferred_element_type=F32)


def _dot(a, b):
    return jnp.dot(a, b, preferred_element_type=F32)


def _silu(z):
    return z / (1.0 + jnp.exp(-z))


def _meta_lane_mask():
    lane = lax.broadcasted_iota(jnp.int32, (1, META_PAD), 1)
    return jnp.where(lane < N_META, 0.0, NEG_INF).astype(F32)


def _compiler_params(n_axes):
    return pltpu.CompilerParams(
        dimension_semantics=("arbitrary",) * n_axes, vmem_limit_bytes=VMEM_LIMIT)


def _alibi_slopes(n_heads):
    return jnp.asarray(np.exp2(-8.0 * np.arange(1, n_heads + 1) / n_heads), F32)


def _inproj_kernel(x_ref, xm_ref, w_ref, o_ref, om_ref, wbf_ref, *, n_q_tiles):
    j = pl.program_id(0)
    i = pl.program_id(1)
    scale = jnp.where(j < n_q_tiles, Q_SCALE_LOG2, 1.0).astype(F32)

    def put(dst, acc):
        for c in range(PROJ_TN // LANES):
            dst[c] = (acc[:, c * LANES:(c + 1) * LANES] * scale).astype(BF16)

    @pl.when(i == 0)
    def _():
        wbf_ref[...] = w_ref[...].astype(BF16)
        put(om_ref, _dot(xm_ref[...].astype(BF16), wbf_ref[...]))

    put(o_ref, _dot(x_ref[...].astype(BF16), wbf_ref[...]))


def _inproj(x, xm, w_stack, layer, q_cols):
    m, d = x.shape
    mm = xm.shape[0]
    n = w_stack.shape[2]
    tm = min(PROJ_X_TILE_BYTES // (d * x.dtype.itemsize), m)
    cb = PROJ_TN // LANES
    assert m % tm == 0 and n % PROJ_TN == 0 and q_cols % PROJ_TN == 0
    kern = functools.partial(_inproj_kernel, n_q_tiles=q_cols // PROJ_TN)
    return pl.pallas_call(
        kern,
        out_shape=(jax.ShapeDtypeStruct((n // LANES, m, LANES), BF16),
                   jax.ShapeDtypeStruct((n // LANES, mm, LANES), BF16)),
        grid=(n // PROJ_TN, m // tm),
        in_specs=[
            pl.BlockSpec((tm, d), lambda j, i: (i, 0)),
            pl.BlockSpec((mm, d), lambda j, i: (0, 0)),
            pl.BlockSpec((None, d, PROJ_TN), lambda j, i: (layer, 0, j)),
        ],
        out_specs=(
            pl.BlockSpec((cb, tm, LANES), lambda j, i: (j, i, 0)),
            pl.BlockSpec((cb, mm, LANES), lambda j, i: (j, 0, 0)),
        ),
        scratch_shapes=[pltpu.VMEM((d, PROJ_TN), BF16)],
        compiler_params=_compiler_params(2),
        name="inproj",
    )(x, xm, w_stack)


def _outproj_ln_kernel(y_ref, ym_ref, w_ref, h_ref, hm_ref, g_ref, b_ref, *refs, emit_bf16):
    *out_refs, wbf_ref = refs

    def layer_norm(y_cb, rows, h):
        y = jnp.concatenate([y_cb[c, rows, :] for c in range(y_cb.shape[0])], axis=1)
        t = ALPHA * h + _dot(y, wbf_ref[...])
        mu = jnp.mean(t, -1, keepdims=True)
        d = t - mu
        var = jnp.mean(d * d, -1, keepdims=True)
        return d * lax.rsqrt(var + LN_EPS) * g_ref[...] + b_ref[...]

    if emit_bf16:
        o_ref, om_ref, obf_ref, ombf_ref = out_refs
    else:
        o_ref, om_ref = out_refs

    @pl.when(pl.program_id(0) == 0)
    def _():
        for r0 in range(0, w_ref.shape[0], LN_TM):
            wbf_ref[r0:r0 + LN_TM, :] = w_ref[r0:r0 + LN_TM, :].astype(BF16)
        r = layer_norm(ym_ref, slice(None), hm_ref[...])
        om_ref[...] = r
        if emit_bf16:
            ombf_ref[...] = r.astype(BF16)

    for t in range(LN_TM // LN_SUB):
        rows = slice(t * LN_SUB, (t + 1) * LN_SUB)
        r = layer_norm(y_ref, rows, h_ref[rows, :])
        o_ref[rows, :] = r
        if emit_bf16:
            obf_ref[rows, :] = r.astype(BF16)


def _outproj_ln(y, ym, w_stack, layer, h, hm, g, b, emit_bf16):
    m, d = h.shape
    mm = hm.shape[0]
    cb = d // LANES
    assert m % LN_TM == 0
    row = pl.BlockSpec((LN_TM, d), lambda i: (i, 0))
    meta = pl.BlockSpec((mm, d), lambda i: (0, 0))
    vec = pl.BlockSpec((1, d), lambda i: (0, 0))
    out_shape = [jax.ShapeDtypeStruct((m, d), F32), jax.ShapeDtypeStruct((mm, d), F32)]
    out_specs = [row, meta]
    if emit_bf16:
        out_shape += [jax.ShapeDtypeStruct((m, d), BF16), jax.ShapeDtypeStruct((mm, d), BF16)]
        out_specs += [row, meta]
    return pl.pallas_call(
        functools.partial(_outproj_ln_kernel, emit_bf16=emit_bf16),
        out_shape=tuple(out_shape),
        grid=(m // LN_TM,),
        in_specs=[
            pl.BlockSpec((cb, LN_TM, LANES), lambda i: (0, i, 0)),
            pl.BlockSpec((cb, mm, LANES), lambda i: (0, 0, 0)),
            pl.BlockSpec((None, d, d), lambda i: (layer, 0, 0), pipeline_mode=pl.Buffered(1)),
            row, meta, vec, vec,
        ],
        out_specs=tuple(out_specs),
        scratch_shapes=[pltpu.VMEM((d, d), BF16)],
        compiler_params=_compiler_params(1),
        name="outproj_ln",
    )(y, ym, w_stack, h, hm, g.reshape(1, d), b.reshape(1, d))


def _na_tables():
    qa = np.arange(NA_GROUP_ROWS)
    ke = np.arange(NA_WIN_ROWS)
    dr_idx = ke[None, :] - NA_ROWS // 2 - qa[:, None] + NA_ROWS - 1
    n_groups = GRID_ROWS // NA_GROUP_ROWS
    oks = []
    for g in range(n_groups):
        r = NA_GROUP_ROWS * g + qa
        r0 = np.clip(r - NA_ROWS // 2, 0, GRID_ROWS - NA_ROWS)
        kr = NA_GROUP_ROWS * g - NA_ROWS // 2 + ke
        oks.append((kr[None, :] >= r0[:, None]) & (kr[None, :] < r0[:, None] + NA_ROWS))
    for g in range(1, n_groups - 1):
        assert np.array_equal(oks[g], oks[1])
    row_ok = np.stack([oks[0], oks[1], oks[n_groups - 1]])
    assert dr_idx[row_ok.any(0)].min() >= 0 and dr_idx[row_ok.any(0)].max() <= 2 * NA_ROWS - 2
    return dr_idx, row_ok


_NA_TABLES = _na_tables()


def _na_build_bias(rpb_ref, toep, comb):
    dr_idx, row_ok = _NA_TABLES
    qc = lax.broadcasted_iota(jnp.int32, (GRID_W, LANES), 0)
    lane = lax.broadcasted_iota(jnp.int32, (GRID_W, LANES), 1)
    kc = lane & (GRID_W - 1)
    c0 = jnp.clip(qc - NA_COLS // 2, 0, GRID_W - NA_COLS)
    col_ok = jnp.logical_and(kc >= c0, kc < c0 + NA_COLS)
    dc_idx = jnp.clip(kc - qc + NA_COLS - 1, 0, LANES - 1)
    for dr in range(2 * NA_ROWS - 1):
        row = jnp.broadcast_to(rpb_ref[dr:dr + 1, :], (GRID_W, LANES))
        toep[dr] = jnp.where(col_ok, LOG2E * jnp.take_along_axis(row, dc_idx, axis=1), NEG_INF)
    masked = jnp.full((GRID_W, LANES), NEG_INF, F32)
    left_half = lane < GRID_W
    meta_tile = jnp.broadcast_to(_meta_lane_mask(), (Q_TILE, META_PAD))
    for kind in range(3):
        for a in range(NA_GROUP_ROWS):
            for ep in range(NA_WIN_ROWS // 2):
                halves = [toep[int(dr_idx[a, e])] if row_ok[kind, a, e] else masked
                          for e in (2 * ep, 2 * ep + 1)]
                comb[kind, a * GRID_W:(a + 1) * GRID_W, ep * LANES:(ep + 1) * LANES] = jnp.where(
                    left_half, halves[0], halves[1])
        comb[kind, :, NA_WIN:NA_KEYS] = meta_tile


def _na_kernel(q_ref, k_ref, v_ref, z_ref, qm_ref, km_ref, vm_ref, zm_ref, rpb_ref,
               y_ref, ym_ref, kpad, vpad, kmp, vmp, toep, comb, *, n_groups):
    b = pl.program_id(1)
    gs = pl.program_id(2)
    heads = q_ref.shape[0]

    @pl.when(jnp.logical_and(b == 0, gs == 0))
    def _():
        for hh in range(heads):
            _na_build_bias(rpb_ref.at[hh], toep, comb.at[hh])

    @pl.when(gs == 0)
    def _():
        zeros = jnp.zeros((NA_PAD, HEAD_DIM), BF16)
        for hh in range(heads):
            for pad, src in ((kpad, k_ref), (vpad, v_ref)):
                pad[hh, 0:NA_PAD] = zeros
                pad[hh, NA_PAD:NA_PAD + SEQ] = src[hh]
                pad[hh, NA_PAD + SEQ:NA_PAD + SEQ + NA_PAD] = zeros
            for pad, src in ((kmp, km_ref), (vmp, vm_ref)):
                pad[hh] = jnp.zeros((META_PAD, HEAD_DIM), BF16)
                pad[hh, 0:N_META] = src[hh]
            s = _dot_nt(qm_ref[hh], kmp[hh]) + _meta_lane_mask()
            p = jnp.exp2(s - jnp.max(s, -1, keepdims=True))
            o = _dot(p.astype(BF16), vmp[hh]) / jnp.sum(p, -1, keepdims=True)
            ym_ref[hh] = (o * _silu(zm_ref[hh].astype(F32))).astype(BF16)

    chains = []
    for ti in range(TILES_PER_STEP):
        g = gs * TILES_PER_STEP + ti
        start = pl.multiple_of(g * Q_TILE, Q_TILE)
        kind = jnp.where(g == 0, 0, jnp.where(g == n_groups - 1, 2, 1))
        rows = slice(ti * Q_TILE, (ti + 1) * Q_TILE)
        for hh in range(heads):
            keys = jnp.concatenate([kpad[hh, pl.ds(start, NA_WIN), :], kmp[hh]], axis=0)
            chains.append((hh, rows, start, _dot_nt(q_ref[hh, rows, :], keys) + comb[hh, kind]))
    for hh, rows, start, s in chains:
        p = jnp.exp2(s - jnp.max(s, -1, keepdims=True))
        vals = jnp.concatenate([vpad[hh, pl.ds(start, NA_WIN), :], vmp[hh]], axis=0)
        o = _dot(p.astype(BF16), vals) / jnp.sum(p, -1, keepdims=True)
        y_ref[hh, rows, :] = (o * _silu(z_ref[hh, rows, :].astype(F32))).astype(BF16)


def _na_attention(qkvz, qkvz_m, rpb, bn):
    hp = NA_HEADS_PER_STEP
    n_groups = SEQ // Q_TILE
    n_steps = n_groups // TILES_PER_STEP
    n_dr, n_dc = rpb.shape[1:]
    rpb_pad = jnp.pad(rpb.astype(F32), ((0, 0), (0, 16 - n_dr), (0, LANES - n_dc)))
    sec = A_HEADS // hp

    def real(s):
        return pl.BlockSpec((hp, Q_STEP, HEAD_DIM), lambda h, b, g: (s * sec + h, b * n_steps + g, 0))

    def full(s):
        return pl.BlockSpec((hp, SEQ, HEAD_DIM), lambda h, b, g: (s * sec + h, b, 0))

    def meta(s):
        return pl.BlockSpec((hp, N_META, HEAD_DIM), lambda h, b, g: (s * sec + h, b, 0))

    return pl.pallas_call(
        functools.partial(_na_kernel, n_groups=n_groups),
        out_shape=(jax.ShapeDtypeStruct((A_HEADS, bn * SEQ, HEAD_DIM), BF16),
                   jax.ShapeDtypeStruct((A_HEADS, bn * N_META, HEAD_DIM), BF16)),
        grid=(sec, bn, n_steps),
        in_specs=[
            real(0), full(1), full(2), real(3), meta(0), meta(1), meta(2), meta(3),
            pl.BlockSpec((hp, 16, LANES), lambda h, b, g: (h, 0, 0)),
        ],
        out_specs=(real(0), meta(0)),
        scratch_shapes=[
            pltpu.VMEM((hp, SEQ + 2 * NA_PAD, HEAD_DIM), BF16),
            pltpu.VMEM((hp, SEQ + 2 * NA_PAD, HEAD_DIM), BF16),
            pltpu.VMEM((hp, META_PAD, HEAD_DIM), BF16),
            pltpu.VMEM((hp, META_PAD, HEAD_DIM), BF16),
            pltpu.VMEM((2 * NA_ROWS - 1, GRID_W, LANES), F32),
            pltpu.VMEM((hp, 3, Q_TILE, NA_KEYS), F32),
        ],
        compiler_params=_compiler_params(3),
        name="na_attention",
    )(qkvz, qkvz, qkvz, qkvz, qkvz_m, qkvz_m, qkvz_m, qkvz_m, rpb_pad)


def _diff_kernel(slopes_ref, q_ref, k_ref, v_ref, z_ref, qm_ref, km_ref, vm_ref, zm_ref,
                 lq1_ref, lk1_ref, lq2_ref, lk2_ref, subg_ref, y_ref, ym_ref,
                 kaug, kmp, vmp, diag, sbuf, *, lambda_init):
    h = pl.program_id(0)
    b = pl.program_id(1)
    ns = pl.program_id(2)
    rows_meta = Q_TILE + N_META
    n_chunk = SEQ // DIFF_KC
    meta_mask = _meta_lane_mask()
    lam = (jnp.exp(jnp.sum(lq1_ref[...] * lk1_ref[...], -1, keepdims=True))
           - jnp.exp(jnp.sum(lq2_ref[...] * lk2_ref[...], -1, keepdims=True)) + lambda_init)
    rate = LOG2E * slopes_ref[h]
    c = jnp.full((1, LANES), rate, F32)
    c1 = c.astype(BF16).astype(F32)
    c2 = (c - c1).astype(BF16).astype(F32)
    c3 = (c - c1 - c2).astype(BF16).astype(F32)

    def rate_piece(lane, first):
        return jnp.where(lane < first + 2, c1, jnp.where(lane < first + 4, c2, c3))

    @pl.when(jnp.logical_and(b == 0, ns == 0))
    def _():
        kpos = lax.broadcasted_iota(jnp.int32, (SEQ, LANES), 0)
        lane = lax.broadcasted_iota(jnp.int32, (SEQ, LANES), 1)
        k_lo = (kpos & (DIFF_KC - 1)).astype(F32)
        k_hi = (kpos - (kpos & (DIFF_KC - 1))).astype(F32)
        pair = jnp.where((lane & 1) == 0, k_hi, k_lo)
        feat = jnp.where(lane < 6, -rate_piece(lane, 0), jnp.where(lane < 12, pair, 0.0))
        for j in range(2):
            kaug[j, :, HEAD_DIM:2 * HEAD_DIM] = feat.astype(BF16)
        qi = lax.broadcasted_iota(jnp.int32, (rows_meta, DIFF_KC), 0)
        kj = lax.broadcasted_iota(jnp.int32, (rows_meta, DIFF_KC), 1)
        diag[...] = jnp.where(qi < Q_TILE, -rate * jnp.abs(qi - kj).astype(F32), 0.0)

    @pl.when(ns == 0)
    def _():
        vmp[...] = jnp.zeros((META_PAD, 2 * HEAD_DIM), BF16)
        for j in range(2):
            kaug[j, :, 0:HEAD_DIM] = k_ref[j]
            kmp[j] = jnp.zeros((META_PAD, HEAD_DIM), BF16)
            kmp[j, 0:N_META] = km_ref[j]
            vmp[0:N_META, j * HEAD_DIM:(j + 1) * HEAD_DIM] = vm_ref[j]

    maps = range(2)

    def score_tile(ti, n, rows):
        row = lax.broadcasted_iota(jnp.int32, (rows, LANES), 0)
        lane = lax.broadcasted_iota(jnp.int32, (rows, LANES), 1)
        pair = jnp.where((lane & 1) == 0, (n * Q_TILE).astype(F32), row.astype(F32))
        qfeat = jnp.where(lane < 6, pair, jnp.where(lane < 12, rate_piece(lane, 6), 0.0))
        qfeat = jnp.where(row < Q_TILE, qfeat, 0.0)
        feat_before = qfeat.astype(BF16)
        feat_after = (-qfeat).astype(BF16)
        feat_diag = jnp.zeros((rows, LANES), BF16)
        q_rows = slice(ti * Q_TILE, (ti + 1) * Q_TILE)
        qs = [q_ref[j, q_rows, :] for j in maps]
        if rows > Q_TILE:
            qs = [jnp.concatenate([qs[j], qm_ref[j]], axis=0) for j in maps]
        mx = [None, None]
        for d in range(n_chunk):
            c = (n + d) & (n_chunk - 1)
            start = pl.multiple_of(c * DIFF_KC, DIFF_KC)
            f = feat_diag if d == 0 else jnp.where(c < n, feat_before, feat_after)
            for j in maps:
                s = _dot_nt(jnp.concatenate([qs[j], f], axis=1), kaug[j, pl.ds(start, DIFF_KC), :])
                if d == 0:
                    s = s + diag[0:rows, :]
                sbuf[ti, j, d, 0:rows, :] = s
                mx[j] = s if mx[j] is None else jnp.maximum(mx[j], s)
        s_m = [_dot_nt(qs[j], kmp[j]) + meta_mask for j in maps]
        m = [jnp.maximum(jnp.max(mx[j], -1, keepdims=True), jnp.max(s_m[j], -1, keepdims=True))
             for j in maps]
        return s_m, m

    def finish_tile(ti, n, rows, s_m, m):
        acc = [None, None]
        o = [None, None]
        for d in range(n_chunk):
            c = (n + d) & (n_chunk - 1)
            start = pl.multiple_of(c * DIFF_KC, DIFF_KC)
            v_chunk = jnp.concatenate(
                [v_ref[0, pl.ds(start, DIFF_KC), :], v_ref[1, pl.ds(start, DIFF_KC), :]], axis=1)
            for j in maps:
                p = jnp.exp2(sbuf[ti, j, d, 0:rows, :] - m[j])
                acc[j] = p if acc[j] is None else acc[j] + p
                part = _dot(p.astype(BF16), v_chunk)
                o[j] = part if o[j] is None else o[j] + part
        outs = []
        for j in maps:
            p_m = jnp.exp2(s_m[j] - m[j])
            l = jnp.sum(acc[j], -1, keepdims=True) + jnp.sum(p_m, -1, keepdims=True)
            outs.append((o[j] + _dot(p_m.astype(BF16), vmp[...])) / l)

        o = outs[0] - lam * outs[1]
        o = o * lax.rsqrt(jnp.mean(o * o, -1, keepdims=True) + RMS_EPS)
        o = o * subg_ref[...] * (1.0 - lambda_init)
        q_rows = slice(ti * Q_TILE, (ti + 1) * Q_TILE)
        z = jnp.concatenate([z_ref[0, q_rows, :], z_ref[1, q_rows, :]], axis=1)
        if rows > Q_TILE:
            z = jnp.concatenate([z, jnp.concatenate([zm_ref[0], zm_ref[1]], axis=1)], axis=0)
        res = (o * _silu(z.astype(F32))).astype(BF16)
        for j in maps:
            y_ref[j, q_rows, :] = res[:Q_TILE, j * HEAD_DIM:(j + 1) * HEAD_DIM]
            if rows > Q_TILE:
                ym_ref[j] = res[Q_TILE:, j * HEAD_DIM:(j + 1) * HEAD_DIM]

    tiles = [(ti, ns * TILES_PER_STEP + ti, rows_meta if ti == 0 else Q_TILE)
             for ti in range(TILES_PER_STEP)]
    stats = [score_tile(*tile) for tile in tiles]
    for tile, (s_m, m) in zip(tiles, stats):
        finish_tile(*tile, s_m, m)


def _diff_attention(qkvz, qkvz_m, lq1, lk1, lq2, lk2, subg, layer_idx, bn):
    n_blocks = SEQ // Q_TILE
    n_steps = n_blocks // TILES_PER_STEP
    assert SEQ // DIFF_KC == n_blocks and Q_TILE == DIFF_KC
    lambda_init = 0.8 - 0.6 * math.exp(-0.3 * layer_idx)

    def real(s):
        return pl.BlockSpec((2, Q_STEP, HEAD_DIM), lambda h, b, n: (s * B_HEADS + h, b * n_steps + n, 0))

    def full(s):
        return pl.BlockSpec((2, SEQ, HEAD_DIM), lambda h, b, n: (s * B_HEADS + h, b, 0))

    def meta(s):
        return pl.BlockSpec((2, N_META, HEAD_DIM), lambda h, b, n: (s * B_HEADS + h, b, 0))

    def vec(width):
        return pl.BlockSpec((1, width), lambda h, b, n: (0, 0))

    rows = Q_TILE + N_META
    return pl.pallas_call(
        functools.partial(_diff_kernel, lambda_init=lambda_init),
        out_shape=(jax.ShapeDtypeStruct((2 * B_HEADS, bn * SEQ, HEAD_DIM), BF16),
                   jax.ShapeDtypeStruct((2 * B_HEADS, bn * N_META, HEAD_DIM), BF16)),
        grid=(B_HEADS, bn, n_steps),
        in_specs=[
            pl.BlockSpec(memory_space=pltpu.SMEM),
            real(0), full(1), full(2), real(3), meta(0), meta(1), meta(2), meta(3),
            vec(HEAD_DIM), vec(HEAD_DIM), vec(HEAD_DIM), vec(HEAD_DIM), vec(2 * HEAD_DIM),
        ],
        out_specs=(real(0), meta(0)),
        scratch_shapes=[
            pltpu.VMEM((2, SEQ, 2 * HEAD_DIM), BF16),
            pltpu.VMEM((2, META_PAD, HEAD_DIM), BF16),
            pltpu.VMEM((META_PAD, 2 * HEAD_DIM), BF16),
            pltpu.VMEM((rows, DIFF_KC), F32),
            pltpu.VMEM((TILES_PER_STEP, 2, SEQ // DIFF_KC, rows, DIFF_KC), F32),
        ],
        compiler_params=_compiler_params(3),
        name="diff_attention",
    )(_alibi_slopes(B_HEADS), qkvz, qkvz, qkvz, qkvz, qkvz_m, qkvz_m, qkvz_m, qkvz_m,
      lq1.reshape(1, -1), lk1.reshape(1, -1), lq2.reshape(1, -1), lk2.reshape(1, -1),
      subg.reshape(1, -1))


def _swa_tables():
    qi = np.arange(Q_TILE)[:, None]
    kj = np.arange(SWA_WIN)[None, :]
    dist = np.abs(qi + C_WINDOW - kj)
    near = dist <= C_WINDOW
    valid = np.stack([near & (kj >= C_WINDOW), near, near & (kj < C_WINDOW + Q_TILE)])
    return dist.astype(np.float32), valid.astype(np.float32)


def _swa_kernel(slopes_ref, sink_ref, q_ref, k_ref, v_ref, z_ref, qm_ref, km_ref, vm_ref, zm_ref,
                dist_ref, valid_ref, y_ref, ym_ref, kpad, vpad, kmp, vmp, comb, *, n_blocks):
    kh = pl.program_id(0)
    b = pl.program_id(1)
    ns = pl.program_id(2)

    def stack_heads(x_ref):
        return jnp.concatenate([x_ref[gq] for gq in range(C_GROUP)], axis=0)

    def meta_sink_tile(gq, rows):
        lane = lax.broadcasted_iota(jnp.int32, (rows, META_PAD), 1)
        sink = LOG2E * sink_ref[kh * C_GROUP + gq]
        return jnp.where(lane < N_META, 0.0, jnp.where(lane == N_META, sink, NEG_INF))

    def gate_and_store(dst_ref, o, z_ref_, rows):
        for gq in range(C_GROUP):
            dst_ref[gq] = (o[gq * rows:(gq + 1) * rows] * _silu(z_ref_[gq].astype(F32))).astype(BF16)

    @pl.when(jnp.logical_and(b == 0, ns == 0))
    def _():
        for gq in range(C_GROUP):
            slope = LOG2E * slopes_ref[kh * C_GROUP + gq]
            rows = slice(gq * Q_TILE, (gq + 1) * Q_TILE)
            for c in range(3):
                comb[c, rows, 0:SWA_WIN] = jnp.where(valid_ref[c] > 0.5, -slope * dist_ref[...], NEG_INF)
                comb[c, rows, SWA_WIN:SWA_KEYS] = meta_sink_tile(gq, Q_TILE)

    @pl.when(ns == 0)
    def _():
        zeros = jnp.zeros((C_WINDOW, HEAD_DIM), BF16)
        for pad, src in ((kpad, k_ref), (vpad, v_ref)):
            pad[0:C_WINDOW] = zeros
            pad[C_WINDOW:C_WINDOW + SEQ] = src[...]
            pad[C_WINDOW + SEQ:SEQ + 2 * C_WINDOW] = zeros
        for pad, src in ((kmp, km_ref), (vmp, vm_ref)):
            pad[...] = jnp.zeros((META_PAD, HEAD_DIM), BF16)
            pad[0:N_META] = src[...]
        mask = jnp.concatenate([meta_sink_tile(gq, N_META) for gq in range(C_GROUP)], axis=0)
        s = _dot_nt(stack_heads(qm_ref), kmp[...]) + mask
        p = jnp.exp2(s - jnp.max(s, -1, keepdims=True))
        o = _dot(p.astype(BF16), vmp[...]) / jnp.sum(p, -1, keepdims=True)
        gate_and_store(ym_ref, o, zm_ref, N_META)

    tiles = []
    for ti in range(TILES_PER_STEP):
        n = ns * TILES_PER_STEP + ti
        start = pl.multiple_of(n * Q_TILE, Q_TILE)
        kind = jnp.where(n == 0, 0, jnp.where(n == n_blocks - 1, 2, 1))
        rows = slice(ti * Q_TILE, (ti + 1) * Q_TILE)
        keys = jnp.concatenate([kpad[pl.ds(start, SWA_WIN), :], kmp[...]], axis=0)
        q = jnp.concatenate([q_ref[gq, rows, :] for gq in range(C_GROUP)], axis=0)
        tiles.append((rows, start, _dot_nt(q, keys) + comb[kind]))
    for rows, start, s_all in tiles:
        vals = jnp.concatenate([vpad[pl.ds(start, SWA_WIN), :], vmp[...]], axis=0)
        for gq in range(C_GROUP):
            s = s_all[gq * Q_TILE:(gq + 1) * Q_TILE]
            p = jnp.exp2(s - jnp.max(s, -1, keepdims=True))
            o = _dot(p.astype(BF16), vals) / jnp.sum(p, -1, keepdims=True)
            y_ref[gq, rows, :] = (o * _silu(z_ref[gq, rows, :].astype(F32))).astype(BF16)


def _swa_attention(qkvz, qkvz_m, sink, bn):
    n_blocks = SEQ // Q_TILE
    n_steps = n_blocks // TILES_PER_STEP
    assert n_blocks >= 2
    k_cb0 = C_HEADS
    v_cb0 = k_cb0 + C_KV_HEADS
    z_grp0 = (v_cb0 + C_KV_HEADS) // C_GROUP
    dist, valid = _swa_tables()

    def q_like(grp0):
        return pl.BlockSpec((C_GROUP, Q_STEP, HEAD_DIM), lambda kh, b, n: (grp0 + kh, b * n_steps + n, 0))

    def q_like_meta(grp0):
        return pl.BlockSpec((C_GROUP, N_META, HEAD_DIM), lambda kh, b, n: (grp0 + kh, b, 0))

    def kv(cb0, rows):
        return pl.BlockSpec((None, rows, HEAD_DIM), lambda kh, b, n: (cb0 + kh, b, 0))

    smem = pl.BlockSpec(memory_space=pltpu.SMEM)
    return pl.pallas_call(
        functools.partial(_swa_kernel, n_blocks=n_blocks),
        out_shape=(jax.ShapeDtypeStruct((C_HEADS, bn * SEQ, HEAD_DIM), BF16),
                   jax.ShapeDtypeStruct((C_HEADS, bn * N_META, HEAD_DIM), BF16)),
        grid=(C_KV_HEADS, bn, n_steps),
        in_specs=[
            smem, smem,
            q_like(0), kv(k_cb0, SEQ), kv(v_cb0, SEQ), q_like(z_grp0),
            q_like_meta(0), kv(k_cb0, N_META), kv(v_cb0, N_META), q_like_meta(z_grp0),
            pl.BlockSpec((Q_TILE, SWA_WIN), lambda kh, b, n: (0, 0)),
            pl.BlockSpec((3, Q_TILE, SWA_WIN), lambda kh, b, n: (0, 0, 0)),
        ],
        out_specs=(q_like(0), q_like_meta(0)),
        scratch_shapes=[
            pltpu.VMEM((SEQ + 2 * C_WINDOW, HEAD_DIM), BF16),
            pltpu.VMEM((SEQ + 2 * C_WINDOW, HEAD_DIM), BF16),
            pltpu.VMEM((META_PAD, HEAD_DIM), BF16),
            pltpu.VMEM((META_PAD, HEAD_DIM), BF16),
            pltpu.VMEM((3, C_GROUP * Q_TILE, SWA_KEYS), F32),
        ],
        compiler_params=_compiler_params(3),
        name="swa_attention",
    )(_alibi_slopes(C_HEADS), sink.astype(F32), qkvz, qkvz, qkvz, qkvz,
      qkvz_m, qkvz_m, qkvz_m, qkvz_m, jnp.asarray(dist), jnp.asarray(valid))


def kernel(x, meta_tokens, w_in_a, rpb_a, w_in_b, lam_q1_b, lam_k1_b, lam_q2_b, lam_k2_b,
           subln_g_b, w_in_c, sink_c, w_out, ln_g, ln_b):
    bn, seq, d = x.shape
    assert seq == SEQ and d == D_MODEL
    h = x.reshape(bn * seq, d)
    hm = jnp.tile(meta_tokens.astype(F32), (bn, 1))
    h_bf, hm_bf = h, hm
    for i in range(DEPTH):
        kind, j = i % 3, i // 3
        if kind == 0:
            qkvz, qkvz_m = _inproj(h_bf, hm_bf, w_in_a, j, A_HEADS * HEAD_DIM)
            y, ym = _na_attention(qkvz, qkvz_m, rpb_a[j], bn)
        elif kind == 1:
            qkvz, qkvz_m = _inproj(h_bf, hm_bf, w_in_b, j, 2 * B_HEADS * HEAD_DIM)
            y, ym = _diff_attention(qkvz, qkvz_m, lam_q1_b[j], lam_k1_b[j], lam_q2_b[j],
                                    lam_k2_b[j], subln_g_b[j], i, bn)
        else:
            qkvz, qkvz_m = _inproj(h_bf, hm_bf, w_in_c, j, C_HEADS * HEAD_DIM)
            y, ym = _swa_attention(qkvz, qkvz_m, sink_c[j], bn)
        last = i == DEPTH - 1
        outs = _outproj_ln(y, ym, w_out, i, h, hm, ln_g[i], ln_b[i], not last)
        if last:
            h, hm = outs
        else:
            h, hm, h_bf, hm_bf = outs
    return h.reshape(bn, seq, d)
```

```python
import functools
import math

import jax
import jax.numpy as jnp
import numpy as np
from jax import lax
from jax.experimental import pallas as pl
from jax.experimental.pallas import tpu as pltpu

F32 = jnp.float32
BF16 = jnp.bfloat16

D_MODEL = 2048
SEQ = 2048
DEPTH = 4
N_META = 16
GRID_W = 64
GRID_ROWS = SEQ // GRID_W
NA_ROWS = 8
NA_COLS = 16
HEAD_DIM = 128
A_HEADS = 16
B_HEADS = 8
C_HEADS = 16
C_KV_HEADS = 4
C_GROUP = C_HEADS // C_KV_HEADS
C_WINDOW = 128
ALPHA = (2 * DEPTH) ** 0.25
LN_EPS = 1e-5
RMS_EPS = 1e-5
NEG_INF = -1e30
LOG2E = math.log2(math.e)
Q_SCALE_LOG2 = LOG2E * HEAD_DIM ** -0.5

LANES = 128
META_PAD = LANES
VMEM_LIMIT = 60 * 1024 * 1024

PROJ_X_TILE_BYTES = 8 * 1024 * 1024
PROJ_TN = 1024
LN_TM = 512
LN_SUB = 128
Q_TILE = 256
NA_TILES = 4
DIFF_TILES = 8
SWA_TILES = 8
NA_HEADS_PER_STEP = 4
NA_GROUP_ROWS = Q_TILE // GRID_W
NA_WIN_ROWS = NA_GROUP_ROWS + NA_ROWS
NA_WIN = NA_WIN_ROWS * GRID_W
NA_KEYS = NA_WIN + META_PAD
NA_PAD = (NA_ROWS // 2) * GRID_W
RPB_ROWS_PAD = 16
SWA_WIN = Q_TILE + 2 * C_WINDOW
SWA_KEYS = SWA_WIN + META_PAD
DIFF_KC = 256


def _dot_nt(a, b):
    return lax.dot_general(a, b, (((1,), (1,)), ((), ())), preferred_element_type=F32)


def _dot(a, b):
    return jnp.dot(a, b, preferred_element_type=F32)


def _silu(z):
    return z / (1.0 + jnp.exp(-z))


def _meta_lane_mask():
    lane = lax.broadcasted_iota(jnp.int32, (1, META_PAD), 1)
    return jnp.where(lane < N_META, 0.0, NEG_INF).astype(F32)


def _compiler_params(n_axes):
    return pltpu.CompilerParams(
        dimension_semantics=("arbitrary",) * n_axes, vmem_limit_bytes=VMEM_LIMIT)


def _alibi_slopes(n_heads):
    return jnp.asarray(np.exp2(-8.0 * np.arange(1, n_heads + 1) / n_heads), F32)


def _inproj_kernel(x_ref, xm_ref, w_ref, o_ref, om_ref, wbf_ref, *, n_q_tiles):
    j = pl.program_id(0)
    i = pl.program_id(1)
    scale = jnp.where(j < n_q_tiles, Q_SCALE_LOG2, 1.0).astype(F32)

    def put(dst, acc):
        for c in range(PROJ_TN // LANES):
            dst[c] = (acc[:, c * LANES:(c + 1) * LANES] * scale).astype(BF16)

    @pl.when(i == 0)
    def _():
        wbf_ref[...] = w_ref[...].astype(BF16)
        put(om_ref, _dot(xm_ref[...].astype(BF16), wbf_ref[...]))

    put(o_ref, _dot(x_ref[...].astype(BF16), wbf_ref[...]))


def _inproj(x, xm, w_stack, layer, q_cols):
    m, d = x.shape
    mm = xm.shape[0]
    n = w_stack.shape[2]
    tm = min(PROJ_X_TILE_BYTES // (d * x.dtype.itemsize), m)
    cb = PROJ_TN // LANES
    assert m % tm == 0 and n % PROJ_TN == 0 and q_cols % PROJ_TN == 0
    kern = functools.partial(_inproj_kernel, n_q_tiles=q_cols // PROJ_TN)
    return pl.pallas_call(
        kern,
        out_shape=(jax.ShapeDtypeStruct((n // LANES, m, LANES), BF16),
                   jax.ShapeDtypeStruct((n // LANES, mm, LANES), BF16)),
        grid=(n // PROJ_TN, m // tm),
        in_specs=[
            pl.BlockSpec((tm, d), lambda j, i: (i, 0)),
            pl.BlockSpec((mm, d), lambda j, i: (0, 0)),
            pl.BlockSpec((None, d, PROJ_TN), lambda j, i: (layer, 0, j)),
        ],
        out_specs=(
            pl.BlockSpec((cb, tm, LANES), lambda j, i: (j, i, 0)),
            pl.BlockSpec((cb, mm, LANES), lambda j, i: (j, 0, 0)),
        ),
        scratch_shapes=[pltpu.VMEM((d, PROJ_TN), BF16)],
        compiler_params=_compiler_params(2),
        name="inproj",
    )(x, xm, w_stack)


def _outproj_ln_kernel(y_ref, ym_ref, w_ref, h_ref, hm_ref, g_ref, b_ref, *refs, emit_bf16):
    *out_refs, wbf_ref = refs

    def layer_norm(y_cb, rows, h):
        y = jnp.concatenate([y_cb[c, rows, :] for c in range(y_cb.shape[0])], axis=1)
        t = ALPHA * h + _dot(y, wbf_ref[...])
        mu = jnp.mean(t, -1, keepdims=True)
        d = t - mu
        var = jnp.mean(d * d, -1, keepdims=True)
        return d * lax.rsqrt(var + LN_EPS) * g_ref[...] + b_ref[...]

    if emit_bf16:
        o_ref, om_ref, obf_ref, ombf_ref = out_refs
    else:
        o_ref, om_ref = out_refs

    @pl.when(pl.program_id(0) == 0)
    def _():
        for r0 in range(0, w_ref.shape[0], LN_TM):
            wbf_ref[r0:r0 + LN_TM, :] = w_ref[r0:r0 + LN_TM, :].astype(BF16)
        r = layer_norm(ym_ref, slice(None), hm_ref[...])
        om_ref[...] = r
        if emit_bf16:
            ombf_ref[...] = r.astype(BF16)

    for t in range(LN_TM // LN_SUB):
        rows = slice(t * LN_SUB, (t + 1) * LN_SUB)
        r = layer_norm(y_ref, rows, h_ref[rows, :])
        o_ref[rows, :] = r
        if emit_bf16:
            obf_ref[rows, :] = r.astype(BF16)


def _outproj_ln(y, ym, w_stack, layer, h, hm, g, b, emit_bf16):
    m, d = h.shape
    mm = hm.shape[0]
    cb = d // LANES
    assert m % LN_TM == 0
    row = pl.BlockSpec((LN_TM, d), lambda i: (i, 0))
    meta = pl.BlockSpec((mm, d), lambda i: (0, 0))
    vec = pl.BlockSpec((1, d), lambda i: (0, 0))
    out_shape = [jax.ShapeDtypeStruct((m, d), F32), jax.ShapeDtypeStruct((mm, d), F32)]
    out_specs = [row, meta]
    if emit_bf16:
        out_shape += [jax.ShapeDtypeStruct((m, d), BF16), jax.ShapeDtypeStruct((mm, d), BF16)]
        out_specs += [row, meta]
    return pl.pallas_call(
        functools.partial(_outproj_ln_kernel, emit_bf16=emit_bf16),
        out_shape=tuple(out_shape),
        grid=(m // LN_TM,),
        in_specs=[
            pl.BlockSpec((cb, LN_TM, LANES), lambda i: (0, i, 0)),
            pl.BlockSpec((cb, mm, LANES), lambda i: (0, 0, 0)),
            pl.BlockSpec((None, d, d), lambda i: (layer, 0, 0), pipeline_mode=pl.Buffered(1)),
            row, meta, vec, vec,
        ],
        out_specs=tuple(out_specs),
        scratch_shapes=[pltpu.VMEM((d, d), BF16)],
        compiler_params=_compiler_params(1),
        name="outproj_ln",
    )(y, ym, w_stack, h, hm, g.reshape(1, d), b.reshape(1, d))


def _na_tables():
    qa = np.arange(NA_GROUP_ROWS)
    ke = np.arange(NA_WIN_ROWS)
    dr_idx = ke[None, :] - NA_ROWS // 2 - qa[:, None] + NA_ROWS - 1
    n_groups = GRID_ROWS // NA_GROUP_ROWS
    oks = []
    for g in range(n_groups):
        r = NA_GROUP_ROWS * g + qa
        r0 = np.clip(r - NA_ROWS // 2, 0, GRID_ROWS - NA_ROWS)
        kr = NA_GROUP_ROWS * g - NA_ROWS // 2 + ke
        oks.append((kr[None, :] >= r0[:, None]) & (kr[None, :] < r0[:, None] + NA_ROWS))
    for g in range(1, n_groups - 1):
        assert np.array_equal(oks[g], oks[1])
    row_ok = np.stack([oks[0], oks[1], oks[n_groups - 1]])
    assert dr_idx[row_ok.any(0)].min() >= 0 and dr_idx[row_ok.any(0)].max() <= 2 * NA_ROWS - 2
    return dr_idx, row_ok


_NA_TABLES = _na_tables()


def _na_build_bias(rpb_ref, toep, comb):
    dr_idx, row_ok = _NA_TABLES
    qc = lax.broadcasted_iota(jnp.int32, (GRID_W, LANES), 0)
    lane = lax.broadcasted_iota(jnp.int32, (GRID_W, LANES), 1)
    kc = lane & (GRID_W - 1)
    c0 = jnp.clip(qc - NA_COLS // 2, 0, GRID_W - NA_COLS)
    col_ok = jnp.logical_and(kc >= c0, kc < c0 + NA_COLS)
    dc_idx = jnp.clip(kc - qc + NA_COLS - 1, 0, LANES - 1)
    for dr in range(2 * NA_ROWS - 1):
        row = jnp.broadcast_to(rpb_ref[dr:dr + 1, :], (GRID_W, LANES))
        toep[dr] = jnp.where(col_ok, LOG2E * jnp.take_along_axis(row, dc_idx, axis=1), NEG_INF)
    masked = jnp.full((GRID_W, LANES), NEG_INF, F32)
    left_half = lane < GRID_W
    meta_tile = jnp.broadcast_to(_meta_lane_mask(), (Q_TILE, META_PAD))
    for kind in range(3):
        for a in range(NA_GROUP_ROWS):
            for ep in range(NA_WIN_ROWS // 2):
                halves = [toep[int(dr_idx[a, e])] if row_ok[kind, a, e] else masked
                          for e in (2 * ep, 2 * ep + 1)]
                comb[kind, a * GRID_W:(a + 1) * GRID_W, ep * LANES:(ep + 1) * LANES] = jnp.where(
                    left_half, halves[0], halves[1])
        comb[kind, :, NA_WIN:NA_KEYS] = meta_tile


def _na_kernel(q_ref, k_ref, v_ref, z_ref, qm_ref, km_ref, vm_ref, zm_ref, rpb_ref,
               y_ref, ym_ref, kpad, vpad, kmp, vmp, toep, comb, *, n_groups):
    b = pl.program_id(1)
    gs = pl.program_id(2)
    heads = q_ref.shape[0]

    @pl.when(jnp.logical_and(b == 0, gs == 0))
    def _():
        for hh in range(heads):
            _na_build_bias(rpb_ref.at[hh], toep, comb.at[hh])

    @pl.when(gs == 0)
    def _():
        zeros = jnp.zeros((NA_PAD, HEAD_DIM), BF16)
        for hh in range(heads):
            for pad, src in ((kpad, k_ref), (vpad, v_ref)):
                pad[hh, 0:NA_PAD] = zeros
                pad[hh, NA_PAD:NA_PAD + SEQ] = src[hh]
                pad[hh, NA_PAD + SEQ:NA_PAD + SEQ + NA_PAD] = zeros
            for pad, src in ((kmp, km_ref), (vmp, vm_ref)):
                pad[hh] = jnp.zeros((META_PAD, HEAD_DIM), BF16)
                pad[hh, 0:N_META] = src[hh]
            s = _dot_nt(qm_ref[hh], kmp[hh]) + _meta_lane_mask()
            p = jnp.exp2(s - jnp.max(s, -1, keepdims=True))
            o = _dot(p.astype(BF16), vmp[hh]) / jnp.sum(p, -1, keepdims=True)
            ym_ref[hh] = (o * _silu(zm_ref[hh].astype(F32))).astype(BF16)

    chains = []
    for ti in range(NA_TILES):
        g = gs * NA_TILES + ti
        start = pl.multiple_of(g * Q_TILE, Q_TILE)
        kind = jnp.where(g == 0, 0, jnp.where(g == n_groups - 1, 2, 1))
        rows = slice(ti * Q_TILE, (ti + 1) * Q_TILE)
        for hh in range(heads):
            keys = jnp.concatenate([kpad[hh, pl.ds(start, NA_WIN), :], kmp[hh]], axis=0)
            chains.append((hh, rows, start, _dot_nt(q_ref[hh, rows, :], keys) + comb[hh, kind]))
    for hh, rows, start, s in chains:
        p = jnp.exp2(s - jnp.max(s, -1, keepdims=True))
        vals = jnp.concatenate([vpad[hh, pl.ds(start, NA_WIN), :], vmp[hh]], axis=0)
        o = _dot(p.astype(BF16), vals) / jnp.sum(p, -1, keepdims=True)
        y_ref[hh, rows, :] = (o * _silu(z_ref[hh, rows, :].astype(F32))).astype(BF16)


def _na_attention(qkvz, qkvz_m, rpb, bn):
    hp = NA_HEADS_PER_STEP
    n_groups = SEQ // Q_TILE
    n_steps = n_groups // NA_TILES
    n_dr, n_dc = rpb.shape[1:]
    rpb_pad = jnp.pad(rpb.astype(F32), ((0, 0), (0, RPB_ROWS_PAD - n_dr), (0, LANES - n_dc)))
    sec = A_HEADS // hp

    def real(s):
        return pl.BlockSpec((hp, NA_TILES * Q_TILE, HEAD_DIM), lambda h, b, g: (s * sec + h, b * n_steps + g, 0))

    def full(s):
        return pl.BlockSpec((hp, SEQ, HEAD_DIM), lambda h, b, g: (s * sec + h, b, 0))

    def meta(s):
        return pl.BlockSpec((hp, N_META, HEAD_DIM), lambda h, b, g: (s * sec + h, b, 0))

    return pl.pallas_call(
        functools.partial(_na_kernel, n_groups=n_groups),
        out_shape=(jax.ShapeDtypeStruct((A_HEADS, bn * SEQ, HEAD_DIM), BF16),
                   jax.ShapeDtypeStruct((A_HEADS, bn * N_META, HEAD_DIM), BF16)),
        grid=(sec, bn, n_steps),
        in_specs=[
            real(0), full(1), full(2), real(3), meta(0), meta(1), meta(2), meta(3),
            pl.BlockSpec((hp, RPB_ROWS_PAD, LANES), lambda h, b, g: (h, 0, 0)),
        ],
        out_specs=(real(0), meta(0)),
        scratch_shapes=[
            pltpu.VMEM((hp, SEQ + 2 * NA_PAD, HEAD_DIM), BF16),
            pltpu.VMEM((hp, SEQ + 2 * NA_PAD, HEAD_DIM), BF16),
            pltpu.VMEM((hp, META_PAD, HEAD_DIM), BF16),
            pltpu.VMEM((hp, META_PAD, HEAD_DIM), BF16),
            pltpu.VMEM((2 * NA_ROWS - 1, GRID_W, LANES), F32),
            pltpu.VMEM((hp, 3, Q_TILE, NA_KEYS), F32),
        ],
        compiler_params=_compiler_params(3),
        name="na_attention",
    )(qkvz, qkvz, qkvz, qkvz, qkvz_m, qkvz_m, qkvz_m, qkvz_m, rpb_pad)


def _diff_kernel(slopes_ref, q_ref, k_ref, v_ref, z_ref, qm_ref, km_ref, vm_ref, zm_ref,
                 lq1_ref, lk1_ref, lq2_ref, lk2_ref, subg_ref, y_ref, ym_ref,
                 kaug, kmp, vmp, diag, sbuf, *, lambda_init):
    h = pl.program_id(0)
    b = pl.program_id(1)
    ns = pl.program_id(2)
    rows_meta = Q_TILE + N_META
    n_chunk = SEQ // DIFF_KC
    meta_mask = _meta_lane_mask()
    lam = (jnp.exp(jnp.sum(lq1_ref[...] * lk1_ref[...], -1, keepdims=True))
           - jnp.exp(jnp.sum(lq2_ref[...] * lk2_ref[...], -1, keepdims=True)) + lambda_init)
    rate = LOG2E * slopes_ref[h]
    c = jnp.full((1, LANES), rate, F32)
    c1 = c.astype(BF16).astype(F32)
    c2 = (c - c1).astype(BF16).astype(F32)
    c3 = (c - c1 - c2).astype(BF16).astype(F32)

    def rate_piece(lane, first):
        return jnp.where(lane < first + 2, c1, jnp.where(lane < first + 4, c2, c3))

    @pl.when(jnp.logical_and(b == 0, ns == 0))
    def _():
        kpos = lax.broadcasted_iota(jnp.int32, (SEQ, LANES), 0)
        lane = lax.broadcasted_iota(jnp.int32, (SEQ, LANES), 1)
        k_lo = (kpos & (DIFF_KC - 1)).astype(F32)
        k_hi = (kpos - (kpos & (DIFF_KC - 1))).astype(F32)
        pair = jnp.where((lane & 1) == 0, k_hi, k_lo)
        feat = jnp.where(lane < 6, -rate_piece(lane, 0), jnp.where(lane < 12, pair, 0.0))
        for j in range(2):
            kaug[j, :, HEAD_DIM:2 * HEAD_DIM] = feat.astype(BF16)
        qi = lax.broadcasted_iota(jnp.int32, (rows_meta, DIFF_KC), 0)
        kj = lax.broadcasted_iota(jnp.int32, (rows_meta, DIFF_KC), 1)
        diag[...] = jnp.where(qi < Q_TILE, -rate * jnp.abs(qi - kj).astype(F32), 0.0)

    @pl.when(ns == 0)
    def _():
        vmp[...] = jnp.zeros((META_PAD, 2 * HEAD_DIM), BF16)
        for j in range(2):
            kaug[j, :, 0:HEAD_DIM] = k_ref[j]
            kmp[j] = jnp.zeros((META_PAD, HEAD_DIM), BF16)
            kmp[j, 0:N_META] = km_ref[j]
            vmp[0:N_META, j * HEAD_DIM:(j + 1) * HEAD_DIM] = vm_ref[j]

    maps = range(2)

    def score_tile(ti, n, rows):
        row = lax.broadcasted_iota(jnp.int32, (rows, LANES), 0)
        lane = lax.broadcasted_iota(jnp.int32, (rows, LANES), 1)
        pair = jnp.where((lane & 1) == 0, (n * Q_TILE).astype(F32), row.astype(F32))
        qfeat = jnp.where(lane < 6, pair, jnp.where(lane < 12, rate_piece(lane, 6), 0.0))
        qfeat = jnp.where(row < Q_TILE, qfeat, 0.0)
        feat_before = qfeat.astype(BF16)
        feat_after = (-qfeat).astype(BF16)
        feat_diag = jnp.zeros((rows, LANES), BF16)
        q_rows = slice(ti * Q_TILE, (ti + 1) * Q_TILE)
        qs = [q_ref[j, q_rows, :] for j in maps]
        if rows > Q_TILE:
            qs = [jnp.concatenate([qs[j], qm_ref[j]], axis=0) for j in maps]
        mx = [None, None]
        for d in range(n_chunk):
            c = (n + d) & (n_chunk - 1)
            start = pl.multiple_of(c * DIFF_KC, DIFF_KC)
            f = feat_diag if d == 0 else jnp.where(c < n, feat_before, feat_after)
            for j in maps:
                s = _dot_nt(jnp.concatenate([qs[j], f], axis=1), kaug[j, pl.ds(start, DIFF_KC), :])
                if d == 0:
                    s = s + diag[0:rows, :]
                sbuf[ti, j, d, 0:rows, :] = s
                mx[j] = s if mx[j] is None else jnp.maximum(mx[j], s)
        s_m = [_dot_nt(qs[j], kmp[j]) + meta_mask for j in maps]
        m = [jnp.maximum(jnp.max(mx[j], -1, keepdims=True), jnp.max(s_m[j], -1, keepdims=True))
             for j in maps]
        return s_m, m

    def finish_tile(ti, n, rows, s_m, m):
        acc = [None, None]
        o = [None, None]
        for d in range(n_chunk):
            c = (n + d) & (n_chunk - 1)
            start = pl.multiple_of(c * DIFF_KC, DIFF_KC)
            v_chunk = jnp.concatenate(
                [v_ref[0, pl.ds(start, DIFF_KC), :], v_ref[1, pl.ds(start, DIFF_KC), :]], axis=1)
            for j in maps:
                p = jnp.exp2(sbuf[ti, j, d, 0:rows, :] - m[j])
                acc[j] = p if acc[j] is None else acc[j] + p
                part = _dot(p.astype(BF16), v_chunk)
                o[j] = part if o[j] is None else o[j] + part
        outs = []
        for j in maps:
            p_m = jnp.exp2(s_m[j] - m[j])
            l = jnp.sum(acc[j], -1, keepdims=True) + jnp.sum(p_m, -1, keepdims=True)
            outs.append((o[j] + _dot(p_m.astype(BF16), vmp[...])) / l)

        o = outs[0] - lam * outs[1]
        o = o * lax.rsqrt(jnp.mean(o * o, -1, keepdims=True) + RMS_EPS)
        o = o * subg_ref[...] * (1.0 - lambda_init)
        q_rows = slice(ti * Q_TILE, (ti + 1) * Q_TILE)
        z = jnp.concatenate([z_ref[0, q_rows, :], z_ref[1, q_rows, :]], axis=1)
        if rows > Q_TILE:
            z = jnp.concatenate([z, jnp.concatenate([zm_ref[0], zm_ref[1]], axis=1)], axis=0)
        res = (o * _silu(z.astype(F32))).astype(BF16)
        for j in maps:
            y_ref[j, q_rows, :] = res[:Q_TILE, j * HEAD_DIM:(j + 1) * HEAD_DIM]
            if rows > Q_TILE:
                ym_ref[j] = res[Q_TILE:, j * HEAD_DIM:(j + 1) * HEAD_DIM]

    tiles = [(ti, ns * DIFF_TILES + ti, rows_meta if ti == 0 else Q_TILE)
             for ti in range(DIFF_TILES)]
    stats = [score_tile(*tiles[0])]
    for k in range(1, len(tiles)):
        stats.append(score_tile(*tiles[k]))
        finish_tile(*tiles[k - 1], *stats[k - 1])
    finish_tile(*tiles[-1], *stats[-1])


def _diff_attention(qkvz, qkvz_m, lq1, lk1, lq2, lk2, subg, layer_idx, bn):
    n_blocks = SEQ // Q_TILE
    n_steps = n_blocks // DIFF_TILES
    assert SEQ // DIFF_KC == n_blocks and Q_TILE == DIFF_KC
    lambda_init = 0.8 - 0.6 * math.exp(-0.3 * layer_idx)

    def real(s):
        return pl.BlockSpec((2, DIFF_TILES * Q_TILE, HEAD_DIM), lambda h, b, n: (s * B_HEADS + h, b * n_steps + n, 0))

    def full(s):
        return pl.BlockSpec((2, SEQ, HEAD_DIM), lambda h, b, n: (s * B_HEADS + h, b, 0))

    def meta(s):
        return pl.BlockSpec((2, N_META, HEAD_DIM), lambda h, b, n: (s * B_HEADS + h, b, 0))

    def vec(width):
        return pl.BlockSpec((1, width), lambda h, b, n: (0, 0))

    rows = Q_TILE + N_META
    return pl.pallas_call(
        functools.partial(_diff_kernel, lambda_init=lambda_init),
        out_shape=(jax.ShapeDtypeStruct((2 * B_HEADS, bn * SEQ, HEAD_DIM), BF16),
                   jax.ShapeDtypeStruct((2 * B_HEADS, bn * N_META, HEAD_DIM), BF16)),
        grid=(B_HEADS, bn, n_steps),
        in_specs=[
            pl.BlockSpec(memory_space=pltpu.SMEM),
            real(0), full(1), full(2), real(3), meta(0), meta(1), meta(2), meta(3),
            vec(HEAD_DIM), vec(HEAD_DIM), vec(HEAD_DIM), vec(HEAD_DIM), vec(2 * HEAD_DIM),
        ],
        out_specs=(real(0), meta(0)),
        scratch_shapes=[
            pltpu.VMEM((2, SEQ, 2 * HEAD_DIM), BF16),
            pltpu.VMEM((2, META_PAD, HEAD_DIM), BF16),
            pltpu.VMEM((META_PAD, 2 * HEAD_DIM), BF16),
            pltpu.VMEM((rows, DIFF_KC), F32),
            pltpu.VMEM((DIFF_TILES, 2, SEQ // DIFF_KC, rows, DIFF_KC), F32),
        ],
        compiler_params=_compiler_params(3),
        name="diff_attention",
    )(_alibi_slopes(B_HEADS), qkvz, qkvz, qkvz, qkvz, qkvz_m, qkvz_m, qkvz_m, qkvz_m,
      lq1.reshape(1, -1), lk1.reshape(1, -1), lq2.reshape(1, -1), lk2.reshape(1, -1),
      subg.reshape(1, -1))


def _swa_tables():
    qi = np.arange(Q_TILE)[:, None]
    kj = np.arange(SWA_WIN)[None, :]
    dist = np.abs(qi + C_WINDOW - kj)
    near = dist <= C_WINDOW
    valid = np.stack([near & (kj >= C_WINDOW), near, near & (kj < C_WINDOW + Q_TILE)])
    return dist.astype(np.float32), valid.astype(np.float32)


def _swa_kernel(slopes_ref, sink_ref, q_ref, k_ref, v_ref, z_ref, qm_ref, km_ref, vm_ref, zm_ref,
                dist_ref, valid_ref, y_ref, ym_ref, kpad, vpad, kmp, vmp, comb, *, n_blocks):
    kh = pl.program_id(0)
    b = pl.program_id(1)
    ns = pl.program_id(2)

    def stack_heads(x_ref):
        return jnp.concatenate([x_ref[gq] for gq in range(C_GROUP)], axis=0)

    def meta_sink_tile(gq, rows):
        lane = lax.broadcasted_iota(jnp.int32, (rows, META_PAD), 1)
        sink = LOG2E * sink_ref[kh * C_GROUP + gq]
        return jnp.where(lane < N_META, 0.0, jnp.where(lane == N_META, sink, NEG_INF))

    def gate_and_store(dst_ref, o, z_ref_, rows):
        for gq in range(C_GROUP):
            dst_ref[gq] = (o[gq * rows:(gq + 1) * rows] * _silu(z_ref_[gq].astype(F32))).astype(BF16)

    @pl.when(jnp.logical_and(b == 0, ns == 0))
    def _():
        for gq in range(C_GROUP):
            slope = LOG2E * slopes_ref[kh * C_GROUP + gq]
            rows = slice(gq * Q_TILE, (gq + 1) * Q_TILE)
            for c in range(3):
                comb[c, rows, 0:SWA_WIN] = jnp.where(valid_ref[c] > 0.5, -slope * dist_ref[...], NEG_INF)
                comb[c, rows, SWA_WIN:SWA_KEYS] = meta_sink_tile(gq, Q_TILE)

    @pl.when(ns == 0)
    def _():
        zeros = jnp.zeros((C_WINDOW, HEAD_DIM), BF16)
        for pad, src in ((kpad, k_ref), (vpad, v_ref)):
            pad[0:C_WINDOW] = zeros
            pad[C_WINDOW:C_WINDOW + SEQ] = src[...]
            pad[C_WINDOW + SEQ:SEQ + 2 * C_WINDOW] = zeros
        for pad, src in ((kmp, km_ref), (vmp, vm_ref)):
            pad[...] = jnp.zeros((META_PAD, HEAD_DIM), BF16)
            pad[0:N_META] = src[...]
        mask = jnp.concatenate([meta_sink_tile(gq, N_META) for gq in range(C_GROUP)], axis=0)
        s = _dot_nt(stack_heads(qm_ref), kmp[...]) + mask
        p = jnp.exp2(s - jnp.max(s, -1, keepdims=True))
        o = _dot(p.astype(BF16), vmp[...]) / jnp.sum(p, -1, keepdims=True)
        gate_and_store(ym_ref, o, zm_ref, N_META)

    tiles = []
    for ti in range(SWA_TILES):
        n = ns * SWA_TILES + ti
        start = pl.multiple_of(n * Q_TILE, Q_TILE)
        kind = jnp.where(n == 0, 0, jnp.where(n == n_blocks - 1, 2, 1))
        rows = slice(ti * Q_TILE, (ti + 1) * Q_TILE)
        keys = jnp.concatenate([kpad[pl.ds(start, SWA_WIN), :], kmp[...]], axis=0)
        q = jnp.concatenate([q_ref[gq, rows, :] for gq in range(C_GROUP)], axis=0)
        tiles.append((rows, start, _dot_nt(q, keys) + comb[kind]))
    for rows, start, s_all in tiles:
        vals = jnp.concatenate([vpad[pl.ds(start, SWA_WIN), :], vmp[...]], axis=0)
        for gq in range(C_GROUP):
            s = s_all[gq * Q_TILE:(gq + 1) * Q_TILE]
            p = jnp.exp2(s - jnp.max(s, -1, keepdims=True))
            o = _dot(p.astype(BF16), vals) / jnp.sum(p, -1, keepdims=True)
            y_ref[gq, rows, :] = (o * _silu(z_ref[gq, rows, :].astype(F32))).astype(BF16)


def _swa_attention(qkvz, qkvz_m, sink, bn):
    n_blocks = SEQ // Q_TILE
    n_steps = n_blocks // SWA_TILES
    assert n_blocks >= 2
    k_cb0 = C_HEADS
    v_cb0 = k_cb0 + C_KV_HEADS
    z_grp0 = (v_cb0 + C_KV_HEADS) // C_GROUP
    dist, valid = _swa_tables()

    def q_like(grp0):
        return pl.BlockSpec((C_GROUP, SWA_TILES * Q_TILE, HEAD_DIM), lambda kh, b, n: (grp0 + kh, b * n_steps + n, 0))

    def q_like_meta(grp0):
        return pl.BlockSpec((C_GROUP, N_META, HEAD_DIM), lambda kh, b, n: (grp0 + kh, b, 0))

    def kv(cb0, rows):
        return pl.BlockSpec((None, rows, HEAD_DIM), lambda kh, b, n: (cb0 + kh, b, 0))

    smem = pl.BlockSpec(memory_space=pltpu.SMEM)
    return pl.pallas_call(
        functools.partial(_swa_kernel, n_blocks=n_blocks),
        out_shape=(jax.ShapeDtypeStruct((C_HEADS, bn * SEQ, HEAD_DIM), BF16),
                   jax.ShapeDtypeStruct((C_HEADS, bn * N_META, HEAD_DIM), BF16)),
        grid=(C_KV_HEADS, bn, n_steps),
        in_specs=[
            smem, smem,
            q_like(0), kv(k_cb0, SEQ), kv(v_cb0, SEQ), q_like(z_grp0),
            q_like_meta(0), kv(k_cb0, N_META), kv(v_cb0, N_META), q_like_meta(z_grp0),
            pl.BlockSpec((Q_TILE, SWA_WIN), lambda kh, b, n: (0, 0)),
            pl.BlockSpec((3, Q_TILE, SWA_WIN), lambda kh, b, n: (0, 0, 0)),
        ],
        out_specs=(q_like(0), q_like_meta(0)),
        scratch_shapes=[
            pltpu.VMEM((SEQ + 2 * C_WINDOW, HEAD_DIM), BF16),
            pltpu.VMEM((SEQ + 2 * C_WINDOW, HEAD_DIM), BF16),
            pltpu.VMEM((META_PAD, HEAD_DIM), BF16),
            pltpu.VMEM((META_PAD, HEAD_DIM), BF16),
            pltpu.VMEM((3, C_GROUP * Q_TILE, SWA_KEYS), F32),
        ],
        compiler_params=_compiler_params(3),
        name="swa_attention",
    )(_alibi_slopes(C_HEADS), sink.astype(F32), qkvz, qkvz, qkvz, qkvz,
      qkvz_m, qkvz_m, qkvz_m, qkvz_m, jnp.asarray(dist), jnp.asarray(valid))


def kernel(x, meta_tokens, w_in_a, rpb_a, w_in_b, lam_q1_b, lam_k1_b, lam_q2_b, lam_k2_b,
           subln_g_b, w_in_c, sink_c, w_out, ln_g, ln_b):
    bn, seq, d = x.shape
    assert seq == SEQ and d == D_MODEL
    h = x.reshape(bn * seq, d)
    hm = jnp.tile(meta_tokens.astype(F32), (bn, 1))
    h_bf, hm_bf = h, hm
    for i in range(DEPTH):
        kind, j = i % 3, i // 3
        if kind == 0:
            qkvz, qkvz_m = _inproj(h_bf, hm_bf, w_in_a, j, A_HEADS * HEAD_DIM)
            y, ym = _na_attention(qkvz, qkvz_m, rpb_a[j], bn)
        elif kind == 1:
            qkvz, qkvz_m = _inproj(h_bf, hm_bf, w_in_b, j, 2 * B_HEADS * HEAD_DIM)
            y, ym = _diff_attention(qkvz, qkvz_m, lam_q1_b[j], lam_k1_b[j], lam_q2_b[j],
                                    lam_k2_b[j], subln_g_b[j], i, bn)
        else:
            qkvz, qkvz_m = _inproj(h_bf, hm_bf, w_in_c, j, C_HEADS * HEAD_DIM)
            y, ym = _swa_attention(qkvz, qkvz_m, sink_c[j], bn)
        last = i == DEPTH - 1
        outs = _outproj_ln(y, ym, w_out, i, h, hm, ln_g[i], ln_b[i], not last)
        if last:
            h, hm = outs
        else:
            h, hm, h_bf, hm_bf = outs
    return h.reshape(bn, seq, d)
```

```python
import functools
import math

import jax
import jax.numpy as jnp
import numpy as np
from jax import lax
from jax.experimental import pallas as pl
from jax.experimental.pallas import tpu as pltpu

F32 = jnp.float32
BF16 = jnp.bfloat16

D_MODEL = 2048
SEQ = 2048
DEPTH = 4
N_META = 16
GRID_W = 64
GRID_ROWS = SEQ // GRID_W
NA_ROWS = 8
NA_COLS = 16
HEAD_DIM = 128
A_HEADS = 16
B_HEADS = 8
C_HEADS = 16
C_KV_HEADS = 4
C_GROUP = C_HEADS // C_KV_HEADS
C_WINDOW = 128
ALPHA = (2 * DEPTH) ** 0.25
LN_EPS = 1e-5
RMS_EPS = 1e-5
NEG_INF = -1e30
LOG2E = math.log2(math.e)
Q_SCALE_LOG2 = LOG2E * HEAD_DIM ** -0.5

LANES = 128
META_PAD = LANES
VMEM_LIMIT = 60 * 1024 * 1024

PROJ_X_TILE_BYTES = 8 * 1024 * 1024
PROJ_TN = 1024
LN_TM = 512
LN_SUB = 128
Q_TILE = 256
NA_TILES = 4
DIFF_TILES = 8
SWA_TILES = 8
NA_HEADS_PER_STEP = 4
NA_GROUP_ROWS = Q_TILE // GRID_W
NA_WIN_ROWS = NA_GROUP_ROWS + NA_ROWS
NA_WIN = NA_WIN_ROWS * GRID_W
NA_KEYS = NA_WIN + META_PAD
NA_PAD = (NA_ROWS // 2) * GRID_W
RPB_ROWS_PAD = 16
SWA_WIN = Q_TILE + 2 * C_WINDOW
SWA_KEYS = SWA_WIN + META_PAD
DIFF_KC = 256


def _dot_nt(a, b):
    return lax.dot_general(a, b, (((1,), (1,)), ((), ())), preferred_element_type=F32)


def _dot(a, b):
    return jnp.dot(a, b, preferred_element_type=F32)


def _silu(z):
    return z / (1.0 + jnp.exp(-z))


def _meta_lane_mask():
    lane = lax.broadcasted_iota(jnp.int32, (1, META_PAD), 1)
    return jnp.where(lane < N_META, 0.0, NEG_INF).astype(F32)


def _compiler_params(n_axes):
    return pltpu.CompilerParams(
        dimension_semantics=("arbitrary",) * n_axes, vmem_limit_bytes=VMEM_LIMIT)


def _alibi_slopes(n_heads):
    return jnp.asarray(np.exp2(-8.0 * np.arange(1, n_heads + 1) / n_heads), F32)


def _inproj_kernel(x_ref, xm_ref, w_ref, o_ref, om_ref, wbf_ref, *, n_q_tiles):
    j = pl.program_id(0)
    i = pl.program_id(1)
    scale = jnp.where(j < n_q_tiles, Q_SCALE_LOG2, 1.0).astype(F32)

    def put(dst, acc):
        for c in range(PROJ_TN // LANES):
            dst[c] = (acc[:, c * LANES:(c + 1) * LANES] * scale).astype(BF16)

    @pl.when(i == 0)
    def _():
        wbf_ref[...] = w_ref[...].astype(BF16)
        put(om_ref, _dot(xm_ref[...].astype(BF16), wbf_ref[...]))

    put(o_ref, _dot(x_ref[...].astype(BF16), wbf_ref[...]))


def _inproj(x, xm, w_stack, layer, q_cols):
    m, d = x.shape
    mm = xm.shape[0]
    n = w_stack.shape[2]
    tm = min(PROJ_X_TILE_BYTES // (d * x.dtype.itemsize), m)
    cb = PROJ_TN // LANES
    assert m % tm == 0 and n % PROJ_TN == 0 and q_cols % PROJ_TN == 0
    kern = functools.partial(_inproj_kernel, n_q_tiles=q_cols // PROJ_TN)
    return pl.pallas_call(
        kern,
        out_shape=(jax.ShapeDtypeStruct((n // LANES, m, LANES), BF16),
                   jax.ShapeDtypeStruct((n // LANES, mm, LANES), BF16)),
        grid=(n // PROJ_TN, m // tm),
        in_specs=[
            pl.BlockSpec((tm, d), lambda j, i: (i, 0)),
            pl.BlockSpec((mm, d), lambda j, i: (0, 0)),
            pl.BlockSpec((None, d, PROJ_TN), lambda j, i: (layer, 0, j)),
        ],
        out_specs=(
            pl.BlockSpec((cb, tm, LANES), lambda j, i: (j, i, 0)),
            pl.BlockSpec((cb, mm, LANES), lambda j, i: (j, 0, 0)),
        ),
        scratch_shapes=[pltpu.VMEM((d, PROJ_TN), BF16)],
        compiler_params=_compiler_params(2),
        name="inproj",
    )(x, xm, w_stack)


def _outproj_ln_kernel(y_ref, ym_ref, w_ref, h_ref, hm_ref, g_ref, b_ref, *refs, emit_bf16):
    *out_refs, wbf_ref = refs

    def layer_norm(y_cb, rows, h):
        y = jnp.concatenate([y_cb[c, rows, :] for c in range(y_cb.shape[0])], axis=1)
        t = ALPHA * h + _dot(y, wbf_ref[...])
        mu = jnp.mean(t, -1, keepdims=True)
        d = t - mu
        var = jnp.mean(d * d, -1, keepdims=True)
        return d * lax.rsqrt(var + LN_EPS) * g_ref[...] + b_ref[...]

    if emit_bf16:
        o_ref, om_ref, obf_ref, ombf_ref = out_refs
    else:
        o_ref, om_ref = out_refs

    @pl.when(pl.program_id(0) == 0)
    def _():
        for r0 in range(0, w_ref.shape[0], LN_TM):
            wbf_ref[r0:r0 + LN_TM, :] = w_ref[r0:r0 + LN_TM, :].astype(BF16)
        r = layer_norm(ym_ref, slice(None), hm_ref[...])
        om_ref[...] = r
        if emit_bf16:
            ombf_ref[...] = r.astype(BF16)

    for t in range(LN_TM // LN_SUB):
        rows = slice(t * LN_SUB, (t + 1) * LN_SUB)
        r = layer_norm(y_ref, rows, h_ref[rows, :])
        o_ref[rows, :] = r
        if emit_bf16:
            obf_ref[rows, :] = r.astype(BF16)


def _outproj_ln(y, ym, w_stack, layer, h, hm, g, b, emit_bf16):
    m, d = h.shape
    mm = hm.shape[0]
    cb = d // LANES
    assert m % LN_TM == 0
    row = pl.BlockSpec((LN_TM, d), lambda i: (i, 0))
    meta = pl.BlockSpec((mm, d), lambda i: (0, 0))
    vec = pl.BlockSpec((1, d), lambda i: (0, 0))
    out_shape = [jax.ShapeDtypeStruct((m, d), F32), jax.ShapeDtypeStruct((mm, d), F32)]
    out_specs = [row, meta]
    if emit_bf16:
        out_shape += [jax.ShapeDtypeStruct((m, d), BF16), jax.ShapeDtypeStruct((mm, d), BF16)]
        out_specs += [row, meta]
    return pl.pallas_call(
        functools.partial(_outproj_ln_kernel, emit_bf16=emit_bf16),
        out_shape=tuple(out_shape),
        grid=(m // LN_TM,),
        in_specs=[
            pl.BlockSpec((cb, LN_TM, LANES), lambda i: (0, i, 0)),
            pl.BlockSpec((cb, mm, LANES), lambda i: (0, 0, 0)),
            pl.BlockSpec((None, d, d), lambda i: (layer, 0, 0), pipeline_mode=pl.Buffered(1)),
            row, meta, vec, vec,
        ],
        out_specs=tuple(out_specs),
        scratch_shapes=[pltpu.VMEM((d, d), BF16)],
        compiler_params=_compiler_params(1),
        name="outproj_ln",
    )(y, ym, w_stack, h, hm, g.reshape(1, d), b.reshape(1, d))


def _na_tables():
    qa = np.arange(NA_GROUP_ROWS)
    ke = np.arange(NA_WIN_ROWS)
    dr_idx = ke[None, :] - NA_ROWS // 2 - qa[:, None] + NA_ROWS - 1
    n_groups = GRID_ROWS // NA_GROUP_ROWS
    oks = []
    for g in range(n_groups):
        r = NA_GROUP_ROWS * g + qa
        r0 = np.clip(r - NA_ROWS // 2, 0, GRID_ROWS - NA_ROWS)
        kr = NA_GROUP_ROWS * g - NA_ROWS // 2 + ke
        oks.append((kr[None, :] >= r0[:, None]) & (kr[None, :] < r0[:, None] + NA_ROWS))
    for g in range(1, n_groups - 1):
        assert np.array_equal(oks[g], oks[1])
    row_ok = np.stack([oks[0], oks[1], oks[n_groups - 1]])
    assert dr_idx[row_ok.any(0)].min() >= 0 and dr_idx[row_ok.any(0)].max() <= 2 * NA_ROWS - 2
    return dr_idx, row_ok


_NA_TABLES = _na_tables()


def _na_build_bias(rpb_ref, toep, comb):
    dr_idx, row_ok = _NA_TABLES
    qc = lax.broadcasted_iota(jnp.int32, (GRID_W, LANES), 0)
    lane = lax.broadcasted_iota(jnp.int32, (GRID_W, LANES), 1)
    kc = lane & (GRID_W - 1)
    c0 = jnp.clip(qc - NA_COLS // 2, 0, GRID_W - NA_COLS)
    col_ok = jnp.logical_and(kc >= c0, kc < c0 + NA_COLS)
    dc_idx = jnp.clip(kc - qc + NA_COLS - 1, 0, LANES - 1)
    for dr in range(2 * NA_ROWS - 1):
        row = jnp.broadcast_to(rpb_ref[dr:dr + 1, :], (GRID_W, LANES))
        toep[dr] = jnp.where(col_ok, LOG2E * jnp.take_along_axis(row, dc_idx, axis=1), NEG_INF)
    masked = jnp.full((GRID_W, LANES), NEG_INF, F32)
    left_half = lane < GRID_W
    meta_tile = jnp.broadcast_to(_meta_lane_mask(), (Q_TILE, META_PAD))
    for kind in range(3):
        for a in range(NA_GROUP_ROWS):
            for ep in range(NA_WIN_ROWS // 2):
                halves = [toep[int(dr_idx[a, e])] if row_ok[kind, a, e] else masked
                          for e in (2 * ep, 2 * ep + 1)]
                comb[kind, a * GRID_W:(a + 1) * GRID_W, ep * LANES:(ep + 1) * LANES] = jnp.where(
                    left_half, halves[0], halves[1])
        comb[kind, :, NA_WIN:NA_KEYS] = meta_tile


def _na_kernel(q_ref, k_ref, v_ref, z_ref, qm_ref, km_ref, vm_ref, zm_ref, rpb_ref,
               y_ref, ym_ref, kpad, vpad, kmp, vmp, toep, comb, *, n_groups):
    b = pl.program_id(1)
    gs = pl.program_id(2)
    heads = q_ref.shape[0]

    @pl.when(jnp.logical_and(b == 0, gs == 0))
    def _():
        for hh in range(heads):
            _na_build_bias(rpb_ref.at[hh], toep, comb.at[hh])

    @pl.when(gs == 0)
    def _():
        zeros = jnp.zeros((NA_PAD, HEAD_DIM), BF16)
        for hh in range(heads):
            for pad, src in ((kpad, k_ref), (vpad, v_ref)):
                pad[hh, 0:NA_PAD] = zeros
                pad[hh, NA_PAD:NA_PAD + SEQ] = src[hh]
                pad[hh, NA_PAD + SEQ:NA_PAD + SEQ + NA_PAD] = zeros
            for pad, src in ((kmp, km_ref), (vmp, vm_ref)):
                pad[hh] = jnp.zeros((META_PAD, HEAD_DIM), BF16)
                pad[hh, 0:N_META] = src[hh]
            s = _dot_nt(qm_ref[hh], kmp[hh]) + _meta_lane_mask()
            p = jnp.exp2(s - jnp.max(s, -1, keepdims=True))
            o = _dot(p.astype(BF16), vmp[hh]) / jnp.sum(p, -1, keepdims=True)
            ym_ref[hh] = (o * _silu(zm_ref[hh].astype(F32))).astype(BF16)

    def score_tile(ti):
        g = gs * NA_TILES + ti
        start = pl.multiple_of(g * Q_TILE, Q_TILE)
        kind = jnp.where(g == 0, 0, jnp.where(g == n_groups - 1, 2, 1))
        rows = slice(ti * Q_TILE, (ti + 1) * Q_TILE)
        scores = []
        for hh in range(heads):
            keys = jnp.concatenate([kpad[hh, pl.ds(start, NA_WIN), :], kmp[hh]], axis=0)
            scores.append(_dot_nt(q_ref[hh, rows, :], keys) + comb[hh, kind])
        return rows, start, scores

    def finish_tile(rows, start, scores):
        for hh, s in enumerate(scores):
            p = jnp.exp2(s - jnp.max(s, -1, keepdims=True))
            vals = jnp.concatenate([vpad[hh, pl.ds(start, NA_WIN), :], vmp[hh]], axis=0)
            o = _dot(p.astype(BF16), vals) / jnp.sum(p, -1, keepdims=True)
            y_ref[hh, rows, :] = (o * _silu(z_ref[hh, rows, :].astype(F32))).astype(BF16)

    pending = score_tile(0)
    for ti in range(1, NA_TILES):
        scored = score_tile(ti)
        finish_tile(*pending)
        pending = scored
    finish_tile(*pending)


def _na_attention(qkvz, qkvz_m, rpb, bn):
    hp = NA_HEADS_PER_STEP
    n_groups = SEQ // Q_TILE
    n_steps = n_groups // NA_TILES
    n_dr, n_dc = rpb.shape[1:]
    rpb_pad = jnp.pad(rpb.astype(F32), ((0, 0), (0, RPB_ROWS_PAD - n_dr), (0, LANES - n_dc)))
    sec = A_HEADS // hp

    def real(s):
        return pl.BlockSpec((hp, NA_TILES * Q_TILE, HEAD_DIM), lambda h, b, g: (s * sec + h, b * n_steps + g, 0))

    def full(s):
        return pl.BlockSpec((hp, SEQ, HEAD_DIM), lambda h, b, g: (s * sec + h, b, 0))

    def meta(s):
        return pl.BlockSpec((hp, N_META, HEAD_DIM), lambda h, b, g: (s * sec + h, b, 0))

    return pl.pallas_call(
        functools.partial(_na_kernel, n_groups=n_groups),
        out_shape=(jax.ShapeDtypeStruct((A_HEADS, bn * SEQ, HEAD_DIM), BF16),
                   jax.ShapeDtypeStruct((A_HEADS, bn * N_META, HEAD_DIM), BF16)),
        grid=(sec, bn, n_steps),
        in_specs=[
            real(0), full(1), full(2), real(3), meta(0), meta(1), meta(2), meta(3),
            pl.BlockSpec((hp, RPB_ROWS_PAD, LANES), lambda h, b, g: (h, 0, 0)),
        ],
        out_specs=(real(0), meta(0)),
        scratch_shapes=[
            pltpu.VMEM((hp, SEQ + 2 * NA_PAD, HEAD_DIM), BF16),
            pltpu.VMEM((hp, SEQ + 2 * NA_PAD, HEAD_DIM), BF16),
            pltpu.VMEM((hp, META_PAD, HEAD_DIM), BF16),
            pltpu.VMEM((hp, META_PAD, HEAD_DIM), BF16),
            pltpu.VMEM((2 * NA_ROWS - 1, GRID_W, LANES), F32),
            pltpu.VMEM((hp, 3, Q_TILE, NA_KEYS), F32),
        ],
        compiler_params=_compiler_params(3),
        name="na_attention",
    )(qkvz, qkvz, qkvz, qkvz, qkvz_m, qkvz_m, qkvz_m, qkvz_m, rpb_pad)


def _diff_kernel(slopes_ref, q_ref, k_ref, v_ref, z_ref, qm_ref, km_ref, vm_ref, zm_ref,
                 lq1_ref, lk1_ref, lq2_ref, lk2_ref, subg_ref, y_ref, ym_ref,
                 kaug, kmp, vmp, diag, sbuf, *, lambda_init):
    h = pl.program_id(0)
    b = pl.program_id(1)
    ns = pl.program_id(2)
    rows_meta = Q_TILE + N_META
    n_chunk = SEQ // DIFF_KC
    meta_mask = _meta_lane_mask()
    lam = (jnp.exp(jnp.sum(lq1_ref[...] * lk1_ref[...], -1, keepdims=True))
           - jnp.exp(jnp.sum(lq2_ref[...] * lk2_ref[...], -1, keepdims=True)) + lambda_init)
    rate = LOG2E * slopes_ref[h]
    c = jnp.full((1, LANES), rate, F32)
    c1 = c.astype(BF16).astype(F32)
    c2 = (c - c1).astype(BF16).astype(F32)
    c3 = (c - c1 - c2).astype(BF16).astype(F32)

    def rate_piece(lane, first):
        return jnp.where(lane < first + 2, c1, jnp.where(lane < first + 4, c2, c3))

    @pl.when(jnp.logical_and(b == 0, ns == 0))
    def _():
        kpos = lax.broadcasted_iota(jnp.int32, (SEQ, LANES), 0)
        lane = lax.broadcasted_iota(jnp.int32, (SEQ, LANES), 1)
        k_lo = (kpos & (DIFF_KC - 1)).astype(F32)
        k_hi = (kpos - (kpos & (DIFF_KC - 1))).astype(F32)
        pair = jnp.where((lane & 1) == 0, k_hi, k_lo)
        feat = jnp.where(lane < 6, -rate_piece(lane, 0), jnp.where(lane < 12, pair, 0.0))
        for j in range(2):
            kaug[j, :, HEAD_DIM:2 * HEAD_DIM] = feat.astype(BF16)
        qi = lax.broadcasted_iota(jnp.int32, (rows_meta, DIFF_KC), 0)
        kj = lax.broadcasted_iota(jnp.int32, (rows_meta, DIFF_KC), 1)
        diag[...] = jnp.where(qi < Q_TILE, -rate * jnp.abs(qi - kj).astype(F32), 0.0)

    @pl.when(ns == 0)
    def _():
        vmp[...] = jnp.zeros((META_PAD, 2 * HEAD_DIM), BF16)
        for j in range(2):
            kaug[j, :, 0:HEAD_DIM] = k_ref[j]
            kmp[j] = jnp.zeros((META_PAD, HEAD_DIM), BF16)
            kmp[j, 0:N_META] = km_ref[j]
            vmp[0:N_META, j * HEAD_DIM:(j + 1) * HEAD_DIM] = vm_ref[j]

    maps = range(2)

    def score_tile(ti, n, rows):
        row = lax.broadcasted_iota(jnp.int32, (rows, LANES), 0)
        lane = lax.broadcasted_iota(jnp.int32, (rows, LANES), 1)
        pair = jnp.where((lane & 1) == 0, (n * Q_TILE).astype(F32), row.astype(F32))
        qfeat = jnp.where(lane < 6, pair, jnp.where(lane < 12, rate_piece(lane, 6), 0.0))
        qfeat = jnp.where(row < Q_TILE, qfeat, 0.0)
        feat_before = qfeat.astype(BF16)
        feat_after = (-qfeat).astype(BF16)
        feat_diag = jnp.zeros((rows, LANES), BF16)
        q_rows = slice(ti * Q_TILE, (ti + 1) * Q_TILE)
        qs = [q_ref[j, q_rows, :] for j in maps]
        if rows > Q_TILE:
            qs = [jnp.concatenate([qs[j], qm_ref[j]], axis=0) for j in maps]
        mx = [None, None]
        for d in range(n_chunk):
            c = (n + d) & (n_chunk - 1)
            start = pl.multiple_of(c * DIFF_KC, DIFF_KC)
            f = feat_diag if d == 0 else jnp.where(c < n, feat_before, feat_after)
            for j in maps:
                s = _dot_nt(jnp.concatenate([qs[j], f], axis=1), kaug[j, pl.ds(start, DIFF_KC), :])
                if d == 0:
                    s = s + diag[0:rows, :]
                sbuf[ti, j, d, 0:rows, :] = s
                mx[j] = s if mx[j] is None else jnp.maximum(mx[j], s)
        s_m = [_dot_nt(qs[j], kmp[j]) + meta_mask for j in maps]
        m = [jnp.maximum(jnp.max(mx[j], -1, keepdims=True), jnp.max(s_m[j], -1, keepdims=True))
             for j in maps]
        return s_m, m

    def finish_tile(ti, n, rows, s_m, m):
        acc = [None, None]
        o = [None, None]
        for d in range(n_chunk):
            c = (n + d) & (n_chunk - 1)
            start = pl.multiple_of(c * DIFF_KC, DIFF_KC)
            v_chunk = jnp.concatenate(
                [v_ref[0, pl.ds(start, DIFF_KC), :], v_ref[1, pl.ds(start, DIFF_KC), :]], axis=1)
            for j in maps:
                p = jnp.exp2(sbuf[ti, j, d, 0:rows, :] - m[j])
                acc[j] = p if acc[j] is None else acc[j] + p
                part = _dot(p.astype(BF16), v_chunk)
                o[j] = part if o[j] is None else o[j] + part
        outs = []
        for j in maps:
            p_m = jnp.exp2(s_m[j] - m[j])
            l = jnp.sum(acc[j], -1, keepdims=True) + jnp.sum(p_m, -1, keepdims=True)
            outs.append((o[j] + _dot(p_m.astype(BF16), vmp[...])) / l)

        o = outs[0] - lam * outs[1]
        o = o * lax.rsqrt(jnp.mean(o * o, -1, keepdims=True) + RMS_EPS)
        o = o * subg_ref[...] * (1.0 - lambda_init)
        q_rows = slice(ti * Q_TILE, (ti + 1) * Q_TILE)
        z = jnp.concatenate([z_ref[0, q_rows, :], z_ref[1, q_rows, :]], axis=1)
        if rows > Q_TILE:
            z = jnp.concatenate([z, jnp.concatenate([zm_ref[0], zm_ref[1]], axis=1)], axis=0)
        res = (o * _silu(z.astype(F32))).astype(BF16)
        for j in maps:
            y_ref[j, q_rows, :] = res[:Q_TILE, j * HEAD_DIM:(j + 1) * HEAD_DIM]
            if rows > Q_TILE:
                ym_ref[j] = res[Q_TILE:, j * HEAD_DIM:(j + 1) * HEAD_DIM]

    tiles = [(ti, ns * DIFF_TILES + ti, rows_meta if ti == 0 else Q_TILE)
             for ti in range(DIFF_TILES)]
    stats = [score_tile(*tiles[0])]
    for k in range(1, len(tiles)):
        stats.append(score_tile(*tiles[k]))
        finish_tile(*tiles[k - 1], *stats[k - 1])
    finish_tile(*tiles[-1], *stats[-1])


def _diff_attention(qkvz, qkvz_m, lq1, lk1, lq2, lk2, subg, layer_idx, bn):
    n_blocks = SEQ // Q_TILE
    n_steps = n_blocks // DIFF_TILES
    assert SEQ // DIFF_KC == n_blocks and Q_TILE == DIFF_KC
    lambda_init = 0.8 - 0.6 * math.exp(-0.3 * layer_idx)

    def real(s):
        return pl.BlockSpec((2, DIFF_TILES * Q_TILE, HEAD_DIM), lambda h, b, n: (s * B_HEADS + h, b * n_steps + n, 0))

    def full(s):
        return pl.BlockSpec((2, SEQ, HEAD_DIM), lambda h, b, n: (s * B_HEADS + h, b, 0))

    def meta(s):
        return pl.BlockSpec((2, N_META, HEAD_DIM), lambda h, b, n: (s * B_HEADS + h, b, 0))

    def vec(width):
        return pl.BlockSpec((1, width), lambda h, b, n: (0, 0))

    rows = Q_TILE + N_META
    return pl.pallas_call(
        functools.partial(_diff_kernel, lambda_init=lambda_init),
        out_shape=(jax.ShapeDtypeStruct((2 * B_HEADS, bn * SEQ, HEAD_DIM), BF16),
                   jax.ShapeDtypeStruct((2 * B_HEADS, bn * N_META, HEAD_DIM), BF16)),
        grid=(B_HEADS, bn, n_steps),
        in_specs=[
            pl.BlockSpec(memory_space=pltpu.SMEM),
            real(0), full(1), full(2), real(3), meta(0), meta(1), meta(2), meta(3),
            vec(HEAD_DIM), vec(HEAD_DIM), vec(HEAD_DIM), vec(HEAD_DIM), vec(2 * HEAD_DIM),
        ],
        out_specs=(real(0), meta(0)),
        scratch_shapes=[
            pltpu.VMEM((2, SEQ, 2 * HEAD_DIM), BF16),
            pltpu.VMEM((2, META_PAD, HEAD_DIM), BF16),
            pltpu.VMEM((META_PAD, 2 * HEAD_DIM), BF16),
            pltpu.VMEM((rows, DIFF_KC), F32),
            pltpu.VMEM((DIFF_TILES, 2, SEQ // DIFF_KC, rows, DIFF_KC), F32),
        ],
        compiler_params=_compiler_params(3),
        name="diff_attention",
    )(_alibi_slopes(B_HEADS), qkvz, qkvz, qkvz, qkvz, qkvz_m, qkvz_m, qkvz_m, qkvz_m,
      lq1.reshape(1, -1), lk1.reshape(1, -1), lq2.reshape(1, -1), lk2.reshape(1, -1),
      subg.reshape(1, -1))


def _swa_tables():
    qi = np.arange(Q_TILE)[:, None]
    kj = np.arange(SWA_WIN)[None, :]
    dist = np.abs(qi + C_WINDOW - kj)
    near = dist <= C_WINDOW
    valid = np.stack([near & (kj >= C_WINDOW), near, near & (kj < C_WINDOW + Q_TILE)])
    return dist.astype(np.float32), valid.astype(np.float32)


def _swa_kernel(slopes_ref, sink_ref, q_ref, k_ref, v_ref, z_ref, qm_ref, km_ref, vm_ref, zm_ref,
                dist_ref, valid_ref, y_ref, ym_ref, kpad, vpad, kmp, vmp, comb, *, n_blocks):
    kh = pl.program_id(0)
    b = pl.program_id(1)
    ns = pl.program_id(2)

    def stack_heads(x_ref):
        return jnp.concatenate([x_ref[gq] for gq in range(C_GROUP)], axis=0)

    def meta_sink_tile(gq, rows):
        lane = lax.broadcasted_iota(jnp.int32, (rows, META_PAD), 1)
        sink = LOG2E * sink_ref[kh * C_GROUP + gq]
        return jnp.where(lane < N_META, 0.0, jnp.where(lane == N_META, sink, NEG_INF))

    def gate_and_store(dst_ref, o, z_ref_, rows):
        for gq in range(C_GROUP):
            dst_ref[gq] = (o[gq * rows:(gq + 1) * rows] * _silu(z_ref_[gq].astype(F32))).astype(BF16)

    @pl.when(jnp.logical_and(b == 0, ns == 0))
    def _():
        for gq in range(C_GROUP):
            slope = LOG2E * slopes_ref[kh * C_GROUP + gq]
            rows = slice(gq * Q_TILE, (gq + 1) * Q_TILE)
            for c in range(3):
                comb[c, rows, 0:SWA_WIN] = jnp.where(valid_ref[c] > 0.5, -slope * dist_ref[...], NEG_INF)
                comb[c, rows, SWA_WIN:SWA_KEYS] = meta_sink_tile(gq, Q_TILE)

    @pl.when(ns == 0)
    def _():
        zeros = jnp.zeros((C_WINDOW, HEAD_DIM), BF16)
        for pad, src in ((kpad, k_ref), (vpad, v_ref)):
            pad[0:C_WINDOW] = zeros
            pad[C_WINDOW:C_WINDOW + SEQ] = src[...]
            pad[C_WINDOW + SEQ:SEQ + 2 * C_WINDOW] = zeros
        for pad, src in ((kmp, km_ref), (vmp, vm_ref)):
            pad[...] = jnp.zeros((META_PAD, HEAD_DIM), BF16)
            pad[0:N_META] = src[...]
        mask = jnp.concatenate([meta_sink_tile(gq, N_META) for gq in range(C_GROUP)], axis=0)
        s = _dot_nt(stack_heads(qm_ref), kmp[...]) + mask
        p = jnp.exp2(s - jnp.max(s, -1, keepdims=True))
        o = _dot(p.astype(BF16), vmp[...]) / jnp.sum(p, -1, keepdims=True)
        gate_and_store(ym_ref, o, zm_ref, N_META)

    tiles = []
    for ti in range(SWA_TILES):
        n = ns * SWA_TILES + ti
        start = pl.multiple_of(n * Q_TILE, Q_TILE)
        kind = jnp.where(n == 0, 0, jnp.where(n == n_blocks - 1, 2, 1))
        rows = slice(ti * Q_TILE, (ti + 1) * Q_TILE)
        keys = jnp.concatenate([kpad[pl.ds(start, SWA_WIN), :], kmp[...]], axis=0)
        q = jnp.concatenate([q_ref[gq, rows, :] for gq in range(C_GROUP)], axis=0)
        tiles.append((rows, start, _dot_nt(q, keys) + comb[kind]))
    for rows, start, s_all in tiles:
        vals = jnp.concatenate([vpad[pl.ds(start, SWA_WIN), :], vmp[...]], axis=0)
        for gq in range(C_GROUP):
            s = s_all[gq * Q_TILE:(gq + 1) * Q_TILE]
            p = jnp.exp2(s - jnp.max(s, -1, keepdims=True))
            o = _dot(p.astype(BF16), vals) / jnp.sum(p, -1, keepdims=True)
            y_ref[gq, rows, :] = (o * _silu(z_ref[gq, rows, :].astype(F32))).astype(BF16)


def _swa_attention(qkvz, qkvz_m, sink, bn):
    n_blocks = SEQ // Q_TILE
    n_steps = n_blocks // SWA_TILES
    assert n_blocks >= 2
    k_cb0 = C_HEADS
    v_cb0 = k_cb0 + C_KV_HEADS
    z_grp0 = (v_cb0 + C_KV_HEADS) // C_GROUP
    dist, valid = _swa_tables()

    def q_like(grp0):
        return pl.BlockSpec((C_GROUP, SWA_TILES * Q_TILE, HEAD_DIM), lambda kh, b, n: (grp0 + kh, b * n_steps + n, 0))

    def q_like_meta(grp0):
        return pl.BlockSpec((C_GROUP, N_META, HEAD_DIM), lambda kh, b, n: (grp0 + kh, b, 0))

    def kv(cb0, rows):
        return pl.BlockSpec((None, rows, HEAD_DIM), lambda kh, b, n: (cb0 + kh, b, 0))

    smem = pl.BlockSpec(memory_space=pltpu.SMEM)
    return pl.pallas_call(
        functools.partial(_swa_kernel, n_blocks=n_blocks),
        out_shape=(jax.ShapeDtypeStruct((C_HEADS, bn * SEQ, HEAD_DIM), BF16),
                   jax.ShapeDtypeStruct((C_HEADS, bn * N_META, HEAD_DIM), BF16)),
        grid=(C_KV_HEADS, bn, n_steps),
        in_specs=[
            smem, smem,
            q_like(0), kv(k_cb0, SEQ), kv(v_cb0, SEQ), q_like(z_grp0),
            q_like_meta(0), kv(k_cb0, N_META), kv(v_cb0, N_META), q_like_meta(z_grp0),
            pl.BlockSpec((Q_TILE, SWA_WIN), lambda kh, b, n: (0, 0)),
            pl.BlockSpec((3, Q_TILE, SWA_WIN), lambda kh, b, n: (0, 0, 0)),
        ],
        out_specs=(q_like(0), q_like_meta(0)),
        scratch_shapes=[
            pltpu.VMEM((SEQ + 2 * C_WINDOW, HEAD_DIM), BF16),
            pltpu.VMEM((SEQ + 2 * C_WINDOW, HEAD_DIM), BF16),
            pltpu.VMEM((META_PAD, HEAD_DIM), BF16),
            pltpu.VMEM((META_PAD, HEAD_DIM), BF16),
            pltpu.VMEM((3, C_GROUP * Q_TILE, SWA_KEYS), F32),
        ],
        compiler_params=_compiler_params(3),
        name="swa_attention",
    )(_alibi_slopes(C_HEADS), sink.astype(F32), qkvz, qkvz, qkvz, qkvz,
      qkvz_m, qkvz_m, qkvz_m, qkvz_m, jnp.asarray(dist), jnp.asarray(valid))


def kernel(x, meta_tokens, w_in_a, rpb_a, w_in_b, lam_q1_b, lam_k1_b, lam_q2_b, lam_k2_b,
           subln_g_b, w_in_c, sink_c, w_out, ln_g, ln_b):
    bn, seq, d = x.shape
    assert seq == SEQ and d == D_MODEL
    h = x.reshape(bn * seq, d)
    hm = jnp.tile(meta_tokens.astype(F32), (bn, 1))
    h_bf, hm_bf = h, hm
    for i in range(DEPTH):
        kind, j = i % 3, i // 3
        if kind == 0:
            qkvz, qkvz_m = _inproj(h_bf, hm_bf, w_in_a, j, A_HEADS * HEAD_DIM)
            y, ym = _na_attention(qkvz, qkvz_m, rpb_a[j], bn)
        elif kind == 1:
            qkvz, qkvz_m = _inproj(h_bf, hm_bf, w_in_b, j, 2 * B_HEADS * HEAD_DIM)
            y, ym = _diff_attention(qkvz, qkvz_m, lam_q1_b[j], lam_k1_b[j], lam_q2_b[j],
                                    lam_k2_b[j], subln_g_b[j], i, bn)
        else:
            qkvz, qkvz_m = _inproj(h_bf, hm_bf, w_in_c, j, C_HEADS * HEAD_DIM)
            y, ym = _swa_attention(qkvz, qkvz_m, sink_c[j], bn)
        last = i == DEPTH - 1
        outs = _outproj_ln(y, ym, w_out, i, h, hm, ln_g[i], ln_b[i], not last)
        if last:
            h, hm = outs
        else:
            h, hm, h_bf, hm_bf = outs
    return h.reshape(bn, seq, d)
```

```python
import functools
import math

import jax
import jax.numpy as jnp
import numpy as np
from jax import lax
from jax.experimental import pallas as pl
from jax.experimental.pallas import tpu as pltpu

F32 = jnp.float32
BF16 = jnp.bfloat16

D_MODEL = 2048
SEQ = 2048
DEPTH = 4
N_META = 16
GRID_W = 64
GRID_ROWS = SEQ // GRID_W
NA_ROWS = 8
NA_COLS = 16
HEAD_DIM = 128
A_HEADS = 16
B_HEADS = 8
C_HEADS = 16
C_KV_HEADS = 4
C_GROUP = C_HEADS // C_KV_HEADS
C_WINDOW = 128
ALPHA = (2 * DEPTH) ** 0.25
LN_EPS = 1e-5
RMS_EPS = 1e-5
NEG_INF = -1e30
LOG2E = math.log2(math.e)
Q_SCALE_LOG2 = LOG2E * HEAD_DIM ** -0.5

LANES = 128
META_PAD = LANES
VMEM_LIMIT = 60 * 1024 * 1024

PROJ_X_TILE_BYTES = 8 * 1024 * 1024
PROJ_TN = 1024
LN_TM = 512
LN_SUB = 128
Q_TILE = 256
NA_TILES = 4
DIFF_TILES = 8
SWA_TILES = 8
NA_HEADS_PER_STEP = 4
NA_GROUP_ROWS = Q_TILE // GRID_W
NA_WIN_ROWS = NA_GROUP_ROWS + NA_ROWS
NA_WIN = NA_WIN_ROWS * GRID_W
NA_KEYS = NA_WIN + META_PAD
NA_PAD = (NA_ROWS // 2) * GRID_W
RPB_ROWS_PAD = 16
SWA_WIN = Q_TILE + 2 * C_WINDOW
SWA_KEYS = SWA_WIN + META_PAD
DIFF_KC = 256


def _dot_nt(a, b):
    return lax.dot_general(a, b, (((1,), (1,)), ((), ())), preferred_element_type=F32)


def _dot(a, b):
    return jnp.dot(a, b, preferred_element_type=F32)


def _silu(z):
    return z / (1.0 + jnp.exp(-z))


def _meta_lane_mask():
    lane = lax.broadcasted_iota(jnp.int32, (1, META_PAD), 1)
    return jnp.where(lane < N_META, 0.0, NEG_INF).astype(F32)


def _compiler_params(n_axes):
    return pltpu.CompilerParams(
        dimension_semantics=("arbitrary",) * n_axes, vmem_limit_bytes=VMEM_LIMIT)


def _alibi_slopes(n_heads):
    return jnp.asarray(np.exp2(-8.0 * np.arange(1, n_heads + 1) / n_heads), F32)


def _inproj_kernel(x_ref, xm_ref, w_ref, o_ref, om_ref, wbf_ref, *, n_q_tiles):
    j = pl.program_id(0)
    i = pl.program_id(1)
    scale = jnp.where(j < n_q_tiles, Q_SCALE_LOG2, 1.0).astype(F32)

    def put(dst, acc):
        for c in range(PROJ_TN // LANES):
            dst[c] = (acc[:, c * LANES:(c + 1) * LANES] * scale).astype(BF16)

    @pl.when(i == 0)
    def _():
        wbf_ref[...] = w_ref[...].astype(BF16)
        put(om_ref, _dot(xm_ref[...].astype(BF16), wbf_ref[...]))

    put(o_ref, _dot(x_ref[...].astype(BF16), wbf_ref[...]))


def _inproj(x, xm, w_stack, layer, q_cols):
    m, d = x.shape
    mm = xm.shape[0]
    n = w_stack.shape[2]
    tm = min(PROJ_X_TILE_BYTES // (d * x.dtype.itemsize), m)
    cb = PROJ_TN // LANES
    assert m % tm == 0 and n % PROJ_TN == 0 and q_cols % PROJ_TN == 0
    kern = functools.partial(_inproj_kernel, n_q_tiles=q_cols // PROJ_TN)
    return pl.pallas_call(
        kern,
        out_shape=(jax.ShapeDtypeStruct((n // LANES, m, LANES), BF16),
                   jax.ShapeDtypeStruct((n // LANES, mm, LANES), BF16)),
        grid=(n // PROJ_TN, m // tm),
        in_specs=[
            pl.BlockSpec((tm, d), lambda j, i: (i, 0)),
            pl.BlockSpec((mm, d), lambda j, i: (0, 0)),
            pl.BlockSpec((None, d, PROJ_TN), lambda j, i: (layer, 0, j)),
        ],
        out_specs=(
            pl.BlockSpec((cb, tm, LANES), lambda j, i: (j, i, 0)),
            pl.BlockSpec((cb, mm, LANES), lambda j, i: (j, 0, 0)),
        ),
        scratch_shapes=[pltpu.VMEM((d, PROJ_TN), BF16)],
        compiler_params=_compiler_params(2),
        name="inproj",
    )(x, xm, w_stack)


def _outproj_ln_kernel(y_ref, ym_ref, w_ref, h_ref, hm_ref, g_ref, b_ref, *refs, emit_bf16):
    *out_refs, wbf_ref = refs

    def layer_norm(y_cb, rows, h):
        y = jnp.concatenate([y_cb[c, rows, :] for c in range(y_cb.shape[0])], axis=1)
        t = ALPHA * h + _dot(y, wbf_ref[...])
        mu = jnp.mean(t, -1, keepdims=True)
        d = t - mu
        var = jnp.mean(d * d, -1, keepdims=True)
        return d * lax.rsqrt(var + LN_EPS) * g_ref[...] + b_ref[...]

    if emit_bf16:
        o_ref, om_ref, obf_ref, ombf_ref = out_refs
    else:
        o_ref, om_ref = out_refs

    @pl.when(pl.program_id(0) == 0)
    def _():
        for r0 in range(0, w_ref.shape[0], LN_TM):
            wbf_ref[r0:r0 + LN_TM, :] = w_ref[r0:r0 + LN_TM, :].astype(BF16)
        r = layer_norm(ym_ref, slice(None), hm_ref[...])
        om_ref[...] = r
        if emit_bf16:
            ombf_ref[...] = r.astype(BF16)

    for t in range(LN_TM // LN_SUB):
        rows = slice(t * LN_SUB, (t + 1) * LN_SUB)
        r = layer_norm(y_ref, rows, h_ref[rows, :])
        o_ref[rows, :] = r
        if emit_bf16:
            obf_ref[rows, :] = r.astype(BF16)


def _outproj_ln(y, ym, w_stack, layer, h, hm, g, b, emit_bf16):
    m, d = h.shape
    mm = hm.shape[0]
    cb = d // LANES
    assert m % LN_TM == 0
    row = pl.BlockSpec((LN_TM, d), lambda i: (i, 0))
    meta = pl.BlockSpec((mm, d), lambda i: (0, 0))
    vec = pl.BlockSpec((1, d), lambda i: (0, 0))
    out_shape = [jax.ShapeDtypeStruct((m, d), F32), jax.ShapeDtypeStruct((mm, d), F32)]
    out_specs = [row, meta]
    if emit_bf16:
        out_shape += [jax.ShapeDtypeStruct((m, d), BF16), jax.ShapeDtypeStruct((mm, d), BF16)]
        out_specs += [row, meta]
    return pl.pallas_call(
        functools.partial(_outproj_ln_kernel, emit_bf16=emit_bf16),
        out_shape=tuple(out_shape),
        grid=(m // LN_TM,),
        in_specs=[
            pl.BlockSpec((cb, LN_TM, LANES), lambda i: (0, i, 0)),
            pl.BlockSpec((cb, mm, LANES), lambda i: (0, 0, 0)),
            pl.BlockSpec((None, d, d), lambda i: (layer, 0, 0), pipeline_mode=pl.Buffered(1)),
            row, meta, vec, vec,
        ],
        out_specs=tuple(out_specs),
        scratch_shapes=[pltpu.VMEM((d, d), BF16)],
        compiler_params=_compiler_params(1),
        name="outproj_ln",
    )(y, ym, w_stack, h, hm, g.reshape(1, d), b.reshape(1, d))


def _na_tables():
    qa = np.arange(NA_GROUP_ROWS)
    ke = np.arange(NA_WIN_ROWS)
    dr_idx = ke[None, :] - NA_ROWS // 2 - qa[:, None] + NA_ROWS - 1
    n_groups = GRID_ROWS // NA_GROUP_ROWS
    oks = []
    for g in range(n_groups):
        r = NA_GROUP_ROWS * g + qa
        r0 = np.clip(r - NA_ROWS // 2, 0, GRID_ROWS - NA_ROWS)
        kr = NA_GROUP_ROWS * g - NA_ROWS // 2 + ke
        oks.append((kr[None, :] >= r0[:, None]) & (kr[None, :] < r0[:, None] + NA_ROWS))
    for g in range(1, n_groups - 1):
        assert np.array_equal(oks[g], oks[1])
    row_ok = np.stack([oks[0], oks[1], oks[n_groups - 1]])
    assert dr_idx[row_ok.any(0)].min() >= 0 and dr_idx[row_ok.any(0)].max() <= 2 * NA_ROWS - 2
    return dr_idx, row_ok


_NA_TABLES = _na_tables()


def _na_build_bias(rpb_ref, toep, comb):
    dr_idx, row_ok = _NA_TABLES
    qc = lax.broadcasted_iota(jnp.int32, (GRID_W, LANES), 0)
    lane = lax.broadcasted_iota(jnp.int32, (GRID_W, LANES), 1)
    kc = lane & (GRID_W - 1)
    c0 = jnp.clip(qc - NA_COLS // 2, 0, GRID_W - NA_COLS)
    col_ok = jnp.logical_and(kc >= c0, kc < c0 + NA_COLS)
    dc_idx = jnp.clip(kc - qc + NA_COLS - 1, 0, LANES - 1)
    for dr in range(2 * NA_ROWS - 1):
        row = jnp.broadcast_to(rpb_ref[dr:dr + 1, :], (GRID_W, LANES))
        toep[dr] = jnp.where(col_ok, LOG2E * jnp.take_along_axis(row, dc_idx, axis=1), NEG_INF)
    masked = jnp.full((GRID_W, LANES), NEG_INF, F32)
    left_half = lane < GRID_W
    meta_tile = jnp.broadcast_to(_meta_lane_mask(), (Q_TILE, META_PAD))
    for kind in range(3):
        for a in range(NA_GROUP_ROWS):
            for ep in range(NA_WIN_ROWS // 2):
                halves = [toep[int(dr_idx[a, e])] if row_ok[kind, a, e] else masked
                          for e in (2 * ep, 2 * ep + 1)]
                comb[kind, a * GRID_W:(a + 1) * GRID_W, ep * LANES:(ep + 1) * LANES] = jnp.where(
                    left_half, halves[0], halves[1])
        comb[kind, :, NA_WIN:NA_KEYS] = meta_tile


def _na_kernel(q_ref, k_ref, v_ref, z_ref, qm_ref, km_ref, vm_ref, zm_ref, rpb_ref,
               y_ref, ym_ref, kpad, vpad, kmp, vmp, toep, comb, *, n_groups):
    b = pl.program_id(1)
    gs = pl.program_id(2)
    heads = q_ref.shape[0]

    @pl.when(jnp.logical_and(b == 0, gs == 0))
    def _():
        for hh in range(heads):
            _na_build_bias(rpb_ref.at[hh], toep, comb.at[hh])

    @pl.when(gs == 0)
    def _():
        zeros = jnp.zeros((NA_PAD, HEAD_DIM), BF16)
        for hh in range(heads):
            for pad, src in ((kpad, k_ref), (vpad, v_ref)):
                pad[hh, 0:NA_PAD] = zeros
                pad[hh, NA_PAD:NA_PAD + SEQ] = src[hh]
                pad[hh, NA_PAD + SEQ:NA_PAD + SEQ + NA_PAD] = zeros
            for pad, src in ((kmp, km_ref), (vmp, vm_ref)):
                pad[hh] = jnp.zeros((META_PAD, HEAD_DIM), BF16)
                pad[hh, 0:N_META] = src[hh]
            s = _dot_nt(qm_ref[hh], kmp[hh]) + _meta_lane_mask()
            p = jnp.exp2(s - jnp.max(s, -1, keepdims=True))
            o = _dot(p.astype(BF16), vmp[hh]) / jnp.sum(p, -1, keepdims=True)
            ym_ref[hh] = (o * _silu(zm_ref[hh].astype(F32))).astype(BF16)

    def score_tile(ti):
        g = gs * NA_TILES + ti
        start = pl.multiple_of(g * Q_TILE, Q_TILE)
        kind = jnp.where(g == 0, 0, jnp.where(g == n_groups - 1, 2, 1))
        rows = slice(ti * Q_TILE, (ti + 1) * Q_TILE)
        scores = []
        for hh in range(heads):
            keys = jnp.concatenate([kpad[hh, pl.ds(start, NA_WIN), :], kmp[hh]], axis=0)
            scores.append(_dot_nt(q_ref[hh, rows, :], keys) + comb[hh, kind])
        return rows, start, scores

    def finish_tile(rows, start, scores):
        for hh, s in enumerate(scores):
            p = jnp.exp2(s - jnp.max(s, -1, keepdims=True))
            vals = jnp.concatenate([vpad[hh, pl.ds(start, NA_WIN), :], vmp[hh]], axis=0)
            o = _dot(p.astype(BF16), vals) / jnp.sum(p, -1, keepdims=True)
            y_ref[hh, rows, :] = (o * _silu(z_ref[hh, rows, :].astype(F32))).astype(BF16)

    pending = score_tile(0)
    for ti in range(1, NA_TILES):
        scored = score_tile(ti)
        finish_tile(*pending)
        pending = scored
    finish_tile(*pending)


def _na_attention(qkvz, qkvz_m, rpb, bn):
    hp = NA_HEADS_PER_STEP
    n_groups = SEQ // Q_TILE
    n_steps = n_groups // NA_TILES
    n_dr, n_dc = rpb.shape[1:]
    rpb_pad = jnp.pad(rpb.astype(F32), ((0, 0), (0, RPB_ROWS_PAD - n_dr), (0, LANES - n_dc)))
    sec = A_HEADS // hp

    def real(s):
        return pl.BlockSpec((hp, NA_TILES * Q_TILE, HEAD_DIM), lambda h, b, g: (s * sec + h, b * n_steps + g, 0))

    def full(s):
        return pl.BlockSpec((hp, SEQ, HEAD_DIM), lambda h, b, g: (s * sec + h, b, 0))

    def meta(s):
        return pl.BlockSpec((hp, N_META, HEAD_DIM), lambda h, b, g: (s * sec + h, b, 0))

    return pl.pallas_call(
        functools.partial(_na_kernel, n_groups=n_groups),
        out_shape=(jax.ShapeDtypeStruct((A_HEADS, bn * SEQ, HEAD_DIM), BF16),
                   jax.ShapeDtypeStruct((A_HEADS, bn * N_META, HEAD_DIM), BF16)),
        grid=(sec, bn, n_steps),
        in_specs=[
            real(0), full(1), full(2), real(3), meta(0), meta(1), meta(2), meta(3),
            pl.BlockSpec((hp, RPB_ROWS_PAD, LANES), lambda h, b, g: (h, 0, 0)),
        ],
        out_specs=(real(0), meta(0)),
        scratch_shapes=[
            pltpu.VMEM((hp, SEQ + 2 * NA_PAD, HEAD_DIM), BF16),
            pltpu.VMEM((hp, SEQ + 2 * NA_PAD, HEAD_DIM), BF16),
            pltpu.VMEM((hp, META_PAD, HEAD_DIM), BF16),
            pltpu.VMEM((hp, META_PAD, HEAD_DIM), BF16),
            pltpu.VMEM((2 * NA_ROWS - 1, GRID_W, LANES), F32),
            pltpu.VMEM((hp, 3, Q_TILE, NA_KEYS), F32),
        ],
        compiler_params=_compiler_params(3),
        name="na_attention",
    )(qkvz, qkvz, qkvz, qkvz, qkvz_m, qkvz_m, qkvz_m, qkvz_m, rpb_pad)


def _diff_kernel(slopes_ref, q_ref, k_ref, v_ref, z_ref, qm_ref, km_ref, vm_ref, zm_ref,
                 lq1_ref, lk1_ref, lq2_ref, lk2_ref, subg_ref, y_ref, ym_ref,
                 kaug, kmp, vmp, diag, sbuf, *, lambda_init):
    h = pl.program_id(0)
    b = pl.program_id(1)
    ns = pl.program_id(2)
    rows_meta = Q_TILE + N_META
    n_chunk = SEQ // DIFF_KC
    meta_mask = _meta_lane_mask()
    lam = (jnp.exp(jnp.sum(lq1_ref[...] * lk1_ref[...], -1, keepdims=True))
           - jnp.exp(jnp.sum(lq2_ref[...] * lk2_ref[...], -1, keepdims=True)) + lambda_init)
    rate = LOG2E * slopes_ref[h]
    c = jnp.full((1, LANES), rate, F32)
    c1 = c.astype(BF16).astype(F32)
    c2 = (c - c1).astype(BF16).astype(F32)
    c3 = (c - c1 - c2).astype(BF16).astype(F32)

    def rate_piece(lane, first):
        return jnp.where(lane < first + 2, c1, jnp.where(lane < first + 4, c2, c3))

    @pl.when(jnp.logical_and(b == 0, ns == 0))
    def _():
        kpos = lax.broadcasted_iota(jnp.int32, (SEQ, LANES), 0)
        lane = lax.broadcasted_iota(jnp.int32, (SEQ, LANES), 1)
        k_lo = (kpos & (DIFF_KC - 1)).astype(F32)
        k_hi = (kpos - (kpos & (DIFF_KC - 1))).astype(F32)
        pair = jnp.where((lane & 1) == 0, k_hi, k_lo)
        feat = jnp.where(lane < 6, -rate_piece(lane, 0), jnp.where(lane < 12, pair, 0.0))
        for j in range(2):
            kaug[j, :, HEAD_DIM:2 * HEAD_DIM] = feat.astype(BF16)
        qi = lax.broadcasted_iota(jnp.int32, (rows_meta, DIFF_KC), 0)
        kj = lax.broadcasted_iota(jnp.int32, (rows_meta, DIFF_KC), 1)
        diag[...] = jnp.where(qi < Q_TILE, -rate * jnp.abs(qi - kj).astype(F32), 0.0)

    @pl.when(ns == 0)
    def _():
        vmp[...] = jnp.zeros((META_PAD, 2 * HEAD_DIM), BF16)
        for j in range(2):
            kaug[j, :, 0:HEAD_DIM] = k_ref[j]
            kmp[j] = jnp.zeros((META_PAD, HEAD_DIM), BF16)
            kmp[j, 0:N_META] = km_ref[j]
            vmp[0:N_META, j * HEAD_DIM:(j + 1) * HEAD_DIM] = vm_ref[j]

    maps = range(2)

    def score_tile(ti, n, rows):
        row = lax.broadcasted_iota(jnp.int32, (rows, LANES), 0)
        lane = lax.broadcasted_iota(jnp.int32, (rows, LANES), 1)
        pair = jnp.where((lane & 1) == 0, (n * Q_TILE).astype(F32), row.astype(F32))
        qfeat = jnp.where(lane < 6, pair, jnp.where(lane < 12, rate_piece(lane, 6), 0.0))
        qfeat = jnp.where(row < Q_TILE, qfeat, 0.0)
        feat_before = qfeat.astype(BF16)
        feat_after = (-qfeat).astype(BF16)
        feat_diag = jnp.zeros((rows, LANES), BF16)
        q_rows = slice(ti * Q_TILE, (ti + 1) * Q_TILE)
        qs = [q_ref[j, q_rows, :] for j in maps]
        if rows > Q_TILE:
            qs = [jnp.concatenate([qs[j], qm_ref[j]], axis=0) for j in maps]
        mx = [None, None]
        for d in range(n_chunk):
            c = (n + d) & (n_chunk - 1)
            start = pl.multiple_of(c * DIFF_KC, DIFF_KC)
            f = feat_diag if d == 0 else jnp.where(c < n, feat_before, feat_after)
            for j in maps:
                s = _dot_nt(jnp.concatenate([qs[j], f], axis=1), kaug[j, pl.ds(start, DIFF_KC), :])
                if d == 0:
                    s = s + diag[0:rows, :]
                sbuf[ti, j, d, 0:rows, :] = s
                mx[j] = s if mx[j] is None else jnp.maximum(mx[j], s)
        s_m = [_dot_nt(qs[j], kmp[j]) + meta_mask for j in maps]
        m = [jnp.maximum(jnp.max(mx[j], -1, keepdims=True), jnp.max(s_m[j], -1, keepdims=True))
             for j in maps]
        return s_m, m

    def finish_tile(ti, n, rows, s_m, m):
        acc = [None, None]
        o = [None, None]
        for d in range(n_chunk):
            c = (n + d) & (n_chunk - 1)
            start = pl.multiple_of(c * DIFF_KC, DIFF_KC)
            v_chunk = jnp.concatenate(
                [v_ref[0, pl.ds(start, DIFF_KC), :], v_ref[1, pl.ds(start, DIFF_KC), :]], axis=1)
            for j in maps:
                p = jnp.exp2(sbuf[ti, j, d, 0:rows, :] - m[j])
                acc[j] = p if acc[j] is None else acc[j] + p
                part = _dot(p.astype(BF16), v_chunk)
                o[j] = part if o[j] is None else o[j] + part
        outs = []
        for j in maps:
            p_m = jnp.exp2(s_m[j] - m[j])
            l = jnp.sum(acc[j], -1, keepdims=True) + jnp.sum(p_m, -1, keepdims=True)
            outs.append((o[j] + _dot(p_m.astype(BF16), vmp[...])) / l)

        o = outs[0] - lam * outs[1]
        o = o * lax.rsqrt(jnp.mean(o * o, -1, keepdims=True) + RMS_EPS)
        o = o * subg_ref[...] * (1.0 - lambda_init)
        q_rows = slice(ti * Q_TILE, (ti + 1) * Q_TILE)
        z = jnp.concatenate([z_ref[0, q_rows, :], z_ref[1, q_rows, :]], axis=1)
        if rows > Q_TILE:
            z = jnp.concatenate([z, jnp.concatenate([zm_ref[0], zm_ref[1]], axis=1)], axis=0)
        res = (o * _silu(z.astype(F32))).astype(BF16)
        for j in maps:
            y_ref[j, q_rows, :] = res[:Q_TILE, j * HEAD_DIM:(j + 1) * HEAD_DIM]
            if rows > Q_TILE:
                ym_ref[j] = res[Q_TILE:, j * HEAD_DIM:(j + 1) * HEAD_DIM]

    tiles = [(ti, ns * DIFF_TILES + ti, rows_meta if ti == 0 else Q_TILE)
             for ti in range(DIFF_TILES)]
    stats = [score_tile(*tiles[0])]
    for k in range(1, len(tiles)):
        stats.append(score_tile(*tiles[k]))
        finish_tile(*tiles[k - 1], *stats[k - 1])
    finish_tile(*tiles[-1], *stats[-1])


def _diff_attention(qkvz, qkvz_m, lq1, lk1, lq2, lk2, subg, layer_idx, bn):
    n_blocks = SEQ // Q_TILE
    n_steps = n_blocks // DIFF_TILES
    assert SEQ // DIFF_KC == n_blocks and Q_TILE == DIFF_KC
    lambda_init = 0.8 - 0.6 * math.exp(-0.3 * layer_idx)

    def real(s):
        return pl.BlockSpec((2, DIFF_TILES * Q_TILE, HEAD_DIM), lambda h, b, n: (s * B_HEADS + h, b * n_steps + n, 0))

    def full(s):
        return pl.BlockSpec((2, SEQ, HEAD_DIM), lambda h, b, n: (s * B_HEADS + h, b, 0))

    def meta(s):
        return pl.BlockSpec((2, N_META, HEAD_DIM), lambda h, b, n: (s * B_HEADS + h, b, 0))

    def vec(width):
        return pl.BlockSpec((1, width), lambda h, b, n: (0, 0))

    rows = Q_TILE + N_META
    return pl.pallas_call(
        functools.partial(_diff_kernel, lambda_init=lambda_init),
        out_shape=(jax.ShapeDtypeStruct((2 * B_HEADS, bn * SEQ, HEAD_DIM), BF16),
                   jax.ShapeDtypeStruct((2 * B_HEADS, bn * N_META, HEAD_DIM), BF16)),
        grid=(B_HEADS, bn, n_steps),
        in_specs=[
            pl.BlockSpec(memory_space=pltpu.SMEM),
            real(0), full(1), full(2), real(3), meta(0), meta(1), meta(2), meta(3),
            vec(HEAD_DIM), vec(HEAD_DIM), vec(HEAD_DIM), vec(HEAD_DIM), vec(2 * HEAD_DIM),
        ],
        out_specs=(real(0), meta(0)),
        scratch_shapes=[
            pltpu.VMEM((2, SEQ, 2 * HEAD_DIM), BF16),
            pltpu.VMEM((2, META_PAD, HEAD_DIM), BF16),
            pltpu.VMEM((META_PAD, 2 * HEAD_DIM), BF16),
            pltpu.VMEM((rows, DIFF_KC), F32),
            pltpu.VMEM((DIFF_TILES, 2, SEQ // DIFF_KC, rows, DIFF_KC), F32),
        ],
        compiler_params=_compiler_params(3),
        name="diff_attention",
    )(_alibi_slopes(B_HEADS), qkvz, qkvz, qkvz, qkvz, qkvz_m, qkvz_m, qkvz_m, qkvz_m,
      lq1.reshape(1, -1), lk1.reshape(1, -1), lq2.reshape(1, -1), lk2.reshape(1, -1),
      subg.reshape(1, -1))


def _swa_tables():
    qi = np.arange(Q_TILE)[:, None]
    kj = np.arange(SWA_WIN)[None, :]
    dist = np.abs(qi + C_WINDOW - kj)
    near = dist <= C_WINDOW
    valid = np.stack([near & (kj >= C_WINDOW), near, near & (kj < C_WINDOW + Q_TILE)])
    return dist.astype(np.float32), valid.astype(np.float32)


def _swa_kernel(slopes_ref, sink_ref, q_ref, k_ref, v_ref, z_ref, qm_ref, km_ref, vm_ref, zm_ref,
                dist_ref, valid_ref, y_ref, ym_ref, kpad, vpad, kmp, vmp, comb, *, n_blocks):
    kh = pl.program_id(0)
    b = pl.program_id(1)
    ns = pl.program_id(2)

    def stack_heads(x_ref):
        return jnp.concatenate([x_ref[gq] for gq in range(C_GROUP)], axis=0)

    def meta_sink_tile(gq, rows):
        lane = lax.broadcasted_iota(jnp.int32, (rows, META_PAD), 1)
        sink = LOG2E * sink_ref[kh * C_GROUP + gq]
        return jnp.where(lane < N_META, 0.0, jnp.where(lane == N_META, sink, NEG_INF))

    def gate_and_store(dst_ref, o, z_ref_, rows):
        for gq in range(C_GROUP):
            dst_ref[gq] = (o[gq * rows:(gq + 1) * rows] * _silu(z_ref_[gq].astype(F32))).astype(BF16)

    @pl.when(jnp.logical_and(b == 0, ns == 0))
    def _():
        for gq in range(C_GROUP):
            slope = LOG2E * slopes_ref[kh * C_GROUP + gq]
            rows = slice(gq * Q_TILE, (gq + 1) * Q_TILE)
            for c in range(3):
                comb[c, rows, 0:SWA_WIN] = jnp.where(valid_ref[c] > 0.5, -slope * dist_ref[...], NEG_INF)
                comb[c, rows, SWA_WIN:SWA_KEYS] = meta_sink_tile(gq, Q_TILE)

    @pl.when(ns == 0)
    def _():
        zeros = jnp.zeros((C_WINDOW, HEAD_DIM), BF16)
        for pad, src in ((kpad, k_ref), (vpad, v_ref)):
            pad[0:C_WINDOW] = zeros
            pad[C_WINDOW:C_WINDOW + SEQ] = src[...]
            pad[C_WINDOW + SEQ:SEQ + 2 * C_WINDOW] = zeros
        for pad, src in ((kmp, km_ref), (vmp, vm_ref)):
            pad[...] = jnp.zeros((META_PAD, HEAD_DIM), BF16)
            pad[0:N_META] = src[...]
        mask = jnp.concatenate([meta_sink_tile(gq, N_META) for gq in range(C_GROUP)], axis=0)
        s = _dot_nt(stack_heads(qm_ref), kmp[...]) + mask
        p = jnp.exp2(s - jnp.max(s, -1, keepdims=True))
        o = _dot(p.astype(BF16), vmp[...]) / jnp.sum(p, -1, keepdims=True)
        gate_and_store(ym_ref, o, zm_ref, N_META)

    tiles = []
    for ti in range(SWA_TILES):
        n = ns * SWA_TILES + ti
        start = pl.multiple_of(n * Q_TILE, Q_TILE)
        kind = jnp.where(n == 0, 0, jnp.where(n == n_blocks - 1, 2, 1))
        rows = slice(ti * Q_TILE, (ti + 1) * Q_TILE)
        keys = jnp.concatenate([kpad[pl.ds(start, SWA_WIN), :], kmp[...]], axis=0)
        q = jnp.concatenate([q_ref[gq, rows, :] for gq in range(C_GROUP)], axis=0)
        tiles.append((rows, start, _dot_nt(q, keys) + comb[kind]))
    for rows, start, s_all in tiles:
        vals = jnp.concatenate([vpad[pl.ds(start, SWA_WIN), :], vmp[...]], axis=0)
        p = jnp.exp2(s_all - jnp.max(s_all, -1, keepdims=True))
        o = _dot(p.astype(BF16), vals) / jnp.sum(p, -1, keepdims=True)
        for gq in range(C_GROUP):
            y_ref[gq, rows, :] = (o[gq * Q_TILE:(gq + 1) * Q_TILE]
                                  * _silu(z_ref[gq, rows, :].astype(F32))).astype(BF16)


def _swa_attention(qkvz, qkvz_m, sink, bn):
    n_blocks = SEQ // Q_TILE
    n_steps = n_blocks // SWA_TILES
    assert n_blocks >= 2
    k_cb0 = C_HEADS
    v_cb0 = k_cb0 + C_KV_HEADS
    z_grp0 = (v_cb0 + C_KV_HEADS) // C_GROUP
    dist, valid = _swa_tables()

    def q_like(grp0):
        return pl.BlockSpec((C_GROUP, SWA_TILES * Q_TILE, HEAD_DIM), lambda kh, b, n: (grp0 + kh, b * n_steps + n, 0))

    def q_like_meta(grp0):
        return pl.BlockSpec((C_GROUP, N_META, HEAD_DIM), lambda kh, b, n: (grp0 + kh, b, 0))

    def kv(cb0, rows):
        return pl.BlockSpec((None, rows, HEAD_DIM), lambda kh, b, n: (cb0 + kh, b, 0))

    smem = pl.BlockSpec(memory_space=pltpu.SMEM)
    return pl.pallas_call(
        functools.partial(_swa_kernel, n_blocks=n_blocks),
        out_shape=(jax.ShapeDtypeStruct((C_HEADS, bn * SEQ, HEAD_DIM), BF16),
                   jax.ShapeDtypeStruct((C_HEADS, bn * N_META, HEAD_DIM), BF16)),
        grid=(C_KV_HEADS, bn, n_steps),
        in_specs=[
            smem, smem,
            q_like(0), kv(k_cb0, SEQ), kv(v_cb0, SEQ), q_like(z_grp0),
            q_like_meta(0), kv(k_cb0, N_META), kv(v_cb0, N_META), q_like_meta(z_grp0),
            pl.BlockSpec((Q_TILE, SWA_WIN), lambda kh, b, n: (0, 0)),
            pl.BlockSpec((3, Q_TILE, SWA_WIN), lambda kh, b, n: (0, 0, 0)),
        ],
        out_specs=(q_like(0), q_like_meta(0)),
        scratch_shapes=[
            pltpu.VMEM((SEQ + 2 * C_WINDOW, HEAD_DIM), BF16),
            pltpu.VMEM((SEQ + 2 * C_WINDOW, HEAD_DIM), BF16),
            pltpu.VMEM((META_PAD, HEAD_DIM), BF16),
            pltpu.VMEM((META_PAD, HEAD_DIM), BF16),
            pltpu.VMEM((3, C_GROUP * Q_TILE, SWA_KEYS), F32),
        ],
        compiler_params=_compiler_params(3),
        name="swa_attention",
    )(_alibi_slopes(C_HEADS), sink.astype(F32), qkvz, qkvz, qkvz, qkvz,
      qkvz_m, qkvz_m, qkvz_m, qkvz_m, jnp.asarray(dist), jnp.asarray(valid))


def kernel(x, meta_tokens, w_in_a, rpb_a, w_in_b, lam_q1_b, lam_k1_b, lam_q2_b, lam_k2_b,
           subln_g_b, w_in_c, sink_c, w_out, ln_g, ln_b):
    bn, seq, d = x.shape
    assert seq == SEQ and d == D_MODEL
    h = x.reshape(bn * seq, d)
    hm = jnp.tile(meta_tokens.astype(F32), (bn, 1))
    h_bf, hm_bf = h, hm
    for i in range(DEPTH):
        kind, j = i % 3, i // 3
        if kind == 0:
            qkvz, qkvz_m = _inproj(h_bf, hm_bf, w_in_a, j, A_HEADS * HEAD_DIM)
            y, ym = _na_attention(qkvz, qkvz_m, rpb_a[j], bn)
        elif kind == 1:
            qkvz, qkvz_m = _inproj(h_bf, hm_bf, w_in_b, j, 2 * B_HEADS * HEAD_DIM)
            y, ym = _diff_attention(qkvz, qkvz_m, lam_q1_b[j], lam_k1_b[j], lam_q2_b[j],
                                    lam_k2_b[j], subln_g_b[j], i, bn)
        else:
            qkvz, qkvz_m = _inproj(h_bf, hm_bf, w_in_c, j, C_HEADS * HEAD_DIM)
            y, ym = _swa_attention(qkvz, qkvz_m, sink_c[j], bn)
        last = i == DEPTH - 1
        outs = _outproj_ln(y, ym, w_out, i, h, hm, ln_g[i], ln_b[i], not last)
        if last:
            h, hm = outs
        else:
            h, hm, h_bf, hm_bf = outs
    return h.reshape(bn, seq, d)
```
